```python
import jax, jax.numpy as jnp
from jax import lax
import numpy as np

D_MODEL = 1024
BATCH = 8
SEQ = 2048
DEPTH = 1

MEM_LEN = 256
EPS = 1e-6

GLA_HEADS = 4
GLA_DV = D_MODEL // (2 * GLA_HEADS)
GLA_DK = GLA_DV // 2
GLA_LOWRANK = 16
GLA_GATE_NORM = 16.0
GLA_CHUNK = 64
CONV_WIDTH = 4
SB_HEADS = 8
SB_DH = D_MODEL // (2 * SB_HEADS)
SB_BLOCK = 128

GLA_QK_W = GLA_HEADS * GLA_DK
GLA_V_W = GLA_HEADS * GLA_DV
SB_W = SB_HEADS * SB_DH
MIX_W = GLA_V_W + SB_W
CONV_W = 2 * GLA_QK_W + GLA_V_W
IN_COLS = CONV_W + GLA_V_W + GLA_LOWRANK + 3 * SB_W

X_HEADS = 4
X_DH = D_MODEL // X_HEADS

PEER_HEADS = 8
PEER_NKEYS = 128
PEER_N = PEER_NKEYS * PEER_NKEYS
PEER_DKEY = 256
PEER_TOPK = 16
PEER_TOKEN_BLOCK = 128

kernel_name = "hybrid_gla_stickbreak_peer_block"


def rms_norm(x, g):
    xf = x.astype(jnp.float32)
    y = xf * lax.rsqrt(jnp.mean(xf * xf, axis=-1, keepdims=True) + EPS)
    return (y * g.astype(jnp.float32)).astype(x.dtype)


def split_heads(a, n_heads):
    b, t, w = a.shape
    return a.reshape(b, t, n_heads, w // n_heads).transpose(0, 2, 1, 3)


def merge_heads(a):
    b, h, t, d = a.shape
    return a.transpose(0, 2, 1, 3).reshape(b, t, h * d)


def causal_short_conv(x, w):
    kw = w.shape[0]
    t = x.shape[1]
    xp = jnp.pad(x, ((0, 0), (kw - 1, 0), (0, 0)))
    y = sum(xp[:, k:k + t] * w[k] for k in range(kw))
    return jax.nn.silu(y)


def gla_chunked(q, k, v, log_decay):
    dtype = v.dtype
    b, h, t, dk = q.shape
    dv = v.shape[-1]
    c = GLA_CHUNK
    n = t // c

    def to_chunks(a):
        return a.astype(jnp.float32).reshape(b, h, n, c, a.shape[-1]).transpose(2, 0, 1, 3, 4)

    qc, kc, vc, gc = map(to_chunks, (q * dk ** -0.5, k, v, log_decay))
    causal = jnp.tril(jnp.ones((c, c), dtype=bool))

    def step(state, inp):
        qi, ki, vi, gi = inp
        cum = jnp.cumsum(gi, axis=2)
        cum_last = cum[:, :, -1, :]
        o_inter = jnp.einsum('bhcd,bhde->bhce', qi * jnp.exp(cum), state)
        diff = cum[:, :, :, None, :] - cum[:, :, None, :, :]
        decay = jnp.exp(jnp.where(causal[:, :, None], diff, -jnp.inf))
        scores = jnp.einsum('bhid,bhjd,bhijd->bhij', qi, ki, decay)
        o_intra = jnp.einsum('bhij,bhje->bhie', scores, vi)
        k_dec = ki * jnp.exp(cum_last[:, :, None, :] - cum)
        new_state = jnp.exp(cum_last)[..., None] * state + jnp.einsum('bhcd,bhce->bhde', k_dec, vi)
        return new_state, o_inter + o_intra

    s0 = jnp.zeros((b, h, dk, dv), jnp.float32)
    _, out = lax.scan(step, s0, (qc, kc, vc, gc))
    return out.transpose(1, 2, 0, 3, 4).reshape(b, h, t, dv).astype(dtype)


def stick_breaking_attention(q, k, v):
    b, h, t, d = q.shape
    scale = d ** -0.5
    outs = []
    for blk in range(t // SB_BLOCK):
        t0 = blk * SB_BLOCK
        t1 = t0 + SB_BLOCK
        qb, kb, vb = q[:, :, t0:t1], k[:, :, :t1], v[:, :, :t1]
        z = jnp.einsum('bhtd,bhsd->bhts', qb, kb).astype(jnp.float32) * scale
        strict = jnp.arange(t1)[None, :] < jnp.arange(t0, t1)[:, None]
        log_beta = jax.nn.log_sigmoid(z)
        log_one_minus = jnp.where(strict, jax.nn.log_sigmoid(-z), 0.0)
        tail = lax.cumsum(log_one_minus, axis=3, reverse=True) - log_one_minus
        weights = jnp.where(strict, jnp.exp(log_beta + tail), 0.0)
        outs.append(jnp.einsum('bhts,bhsd->bhtd', weights.astype(vb.dtype), vb))
    return jnp.concatenate(outs, axis=2)


def memory_cross_attention(hn, mn, w_q, w_kv, w_o):
    b, t, _ = hn.shape
    m = mn.shape[1]
    q = (hn @ w_q).reshape(b, t, X_HEADS, X_DH)
    k, v = jnp.split(mn @ w_kv, 2, axis=-1)
    k = k.reshape(b, m, X_HEADS, X_DH)
    v = v.reshape(b, m, X_HEADS, X_DH)
    s = jnp.einsum('bthd,bmhd->bhtm', q, k).astype(jnp.float32) * X_DH ** -0.5
    p = jax.nn.softmax(s, axis=-1).astype(v.dtype)
    o = jnp.einsum('bhtm,bmhd->bthd', p, v).reshape(b, t, X_HEADS * X_DH)
    return o @ w_o


def peer_ffn(xn, w_q, sub_keys, u, v):
    b, t, d = xn.shape
    q = (xn @ w_q).reshape(b, t, PEER_HEADS, 2, PEER_DKEY // 2)
    s = jnp.einsum('bthpk,hpnk->bthpn', q, sub_keys).astype(jnp.float32)
    top_s, top_i = lax.top_k(s, PEER_TOPK)
    cand_s = top_s[..., 0, :, None] + top_s[..., 1, None, :]
    cand_i = top_i[..., 0, :, None] * PEER_NKEYS + top_i[..., 1, None, :]
    cand_s = cand_s.reshape(b, t, PEER_HEADS, PEER_TOPK * PEER_TOPK)
    cand_i = cand_i.reshape(b, t, PEER_HEADS, PEER_TOPK * PEER_TOPK)
    best_s, best_pos = lax.top_k(cand_s, PEER_TOPK)
    expert_idx = jnp.take_along_axis(cand_i, best_pos, axis=-1)
    gates = jax.nn.softmax(best_s, axis=-1).astype(xn.dtype)

    hk = PEER_HEADS * PEER_TOPK
    n_blk = (b * t) // PEER_TOKEN_BLOCK
    xf = xn.reshape(n_blk, PEER_TOKEN_BLOCK, d)
    idx = expert_idx.reshape(n_blk, PEER_TOKEN_BLOCK, hk)
    gf = gates.reshape(n_blk, PEER_TOKEN_BLOCK, hk)

    def block(args):
        xb, ib, gb = args
        act = jax.nn.gelu(jnp.einsum('td,ted->te', xb, u[ib]))
        return jnp.einsum('te,ted->td', gb * act, v[ib])

    out = lax.map(block, (xf, idx, gf))
    return out.reshape(b, t, d)


def setup_inputs(seed: int = 0) -> dict:
    key = jax.random.key(seed)
    ks = jax.random.split(key, 21)
    f32 = jnp.float32

    def nrm(k, shape, scale):
        return jax.random.normal(k, shape, f32) * scale

    def gain(k, shape):
        return 1.0 + 0.02 * jax.random.normal(k, shape, f32)

    return {
        "x": nrm(ks[0], (BATCH, SEQ, D_MODEL), 1.0),
        "mem": nrm(ks[1], (BATCH, MEM_LEN, D_MODEL), 1.0),
        "mix_norm_g": gain(ks[2], (DEPTH, D_MODEL)),
        "w_in": nrm(ks[3], (DEPTH, D_MODEL, IN_COLS), D_MODEL ** -0.5),
        "gla_conv_w": nrm(ks[4], (DEPTH, CONV_WIDTH, CONV_W), CONV_WIDTH ** -0.5),
        "gla_gate_w": nrm(ks[5], (DEPTH, GLA_LOWRANK, GLA_QK_W), GLA_LOWRANK ** -0.5),
        "gla_gate_b": nrm(ks[6], (DEPTH, GLA_QK_W), 0.1),
        "gla_out_norm_g": gain(ks[7], (DEPTH, GLA_DV)),
        "sb_out_norm_g": gain(ks[8], (DEPTH, SB_W)),
        "w_out": nrm(ks[9], (DEPTH, MIX_W, D_MODEL), MIX_W ** -0.5),
        "xattn_norm_g": gain(ks[10], (DEPTH, D_MODEL)),
        "mem_norm_g": gain(ks[11], (DEPTH, D_MODEL)),
        "w_xq": nrm(ks[12], (DEPTH, D_MODEL, X_HEADS * X_DH), D_MODEL ** -0.5),
        "w_xkv": nrm(ks[13], (DEPTH, D_MODEL, 2 * X_HEADS * X_DH), D_MODEL ** -0.5),
        "w_xo": nrm(ks[14], (DEPTH, X_HEADS * X_DH, D_MODEL), (X_HEADS * X_DH) ** -0.5),
        "ffn_norm_g": gain(ks[15], (DEPTH, D_MODEL)),
        "peer_w_q": nrm(ks[16], (DEPTH, D_MODEL, PEER_HEADS * PEER_DKEY), D_MODEL ** -0.5),
        "peer_sub_keys": nrm(ks[17], (DEPTH, PEER_HEADS, 2, PEER_NKEYS, PEER_DKEY // 2), (PEER_DKEY // 2) ** -0.5),
        "peer_u": nrm(ks[18], (DEPTH, PEER_N, D_MODEL), D_MODEL ** -0.5),
        "peer_v": nrm(ks[19], (DEPTH, PEER_N, D_MODEL), PEER_HEADS ** -0.5),
        "final_norm_g": gain(ks[20], (D_MODEL,)),
    }


def reference(x, mem, mix_norm_g, w_in, gla_conv_w, gla_gate_w, gla_gate_b, gla_out_norm_g,
              sb_out_norm_g, w_out, xattn_norm_g, mem_norm_g, w_xq, w_xkv, w_xo, ffn_norm_g,
              peer_w_q, peer_sub_keys, peer_u, peer_v, final_norm_g):
    h = x
    c1 = CONV_W
    c2 = c1 + GLA_V_W
    c3 = c2 + GLA_LOWRANK
    c4 = c3 + SB_W
    c5 = c4 + SB_W
    for l in range(DEPTH):
        xn = rms_norm(h, mix_norm_g[l])
        proj = xn @ w_in[l]
        gla_qkv, gla_og, gla_low, sb_q, sb_k, sb_v = jnp.split(proj, [c1, c2, c3, c4, c5], axis=-1)

        gla_qkv = causal_short_conv(gla_qkv, gla_conv_w[l])
        g_q, g_k, g_v = jnp.split(gla_qkv, [GLA_QK_W, 2 * GLA_QK_W], axis=-1)
        log_decay = jax.nn.log_sigmoid((gla_low @ gla_gate_w[l] + gla_gate_b[l]).astype(jnp.float32)) / GLA_GATE_NORM
        o_gla = gla_chunked(split_heads(g_q, GLA_HEADS), split_heads(g_k, GLA_HEADS),
                            split_heads(g_v, GLA_HEADS), split_heads(log_decay, GLA_HEADS))
        o_gla = merge_heads(rms_norm(o_gla, gla_out_norm_g[l])) * jax.nn.silu(gla_og)

        o_sb = stick_breaking_attention(split_heads(sb_q, SB_HEADS), split_heads(sb_k, SB_HEADS),
                                        split_heads(sb_v, SB_HEADS))
        o_sb = rms_norm(merge_heads(o_sb), sb_out_norm_g[l])

        h = h + jnp.concatenate([o_gla, o_sb], axis=-1) @ w_out[l]

        h = h + memory_cross_attention(rms_norm(h, xattn_norm_g[l]), rms_norm(mem, mem_norm_g[l]),
                                       w_xq[l], w_xkv[l], w_xo[l])

        h = h + peer_ffn(rms_norm(h, ffn_norm_g[l]), peer_w_q[l], peer_sub_keys[l], peer_u[l], peer_v[l])
    return rms_norm(h, final_norm_g)
```

```python
import functools

import jax
import jax.numpy as jnp
from jax import lax
from jax.experimental import pallas as pl
from jax.experimental.pallas import tpu as pltpu

EPS = 1e-6
GLA_HEADS = 4
GLA_DK = 64
GLA_DV = 128
GLA_LOWRANK = 16
GLA_GATE_NORM = 16.0
GLA_CHUNK = 64
CONV_WIDTH = 4
SB_HEADS = 8
SB_DH = 64
X_HEADS = 4
X_DH = 256
PEER_HEADS = 8
PEER_NKEYS = 128
PEER_TOPK = 16

V7X_LANES = 128
V7X_SUBLANES = 8
V7X_VMEM_LIMIT_BYTES = 52 * 1024 * 1024

F32 = jnp.float32
BF16 = jnp.bfloat16
NEG_INF = float("-inf")


def _params(*semantics):
    return pltpu.CompilerParams(dimension_semantics=semantics, vmem_limit_bytes=V7X_VMEM_LIMIT_BYTES)


def _dot(a, b):
    return jnp.dot(a, b, preferred_element_type=F32)


def _dot_nt(a, b):
    return lax.dot_general(a, b, (((1,), (1,)), ((), ())), preferred_element_type=F32)


def _dot_tn(a, b):
    return lax.dot_general(a, b, (((0,), (0,)), ((), ())), preferred_element_type=F32)


def _rms(x, g):
    return x * lax.rsqrt(jnp.mean(x * x, axis=-1, keepdims=True) + EPS) * g


def _sigmoid(x):
    return 1.0 / (1.0 + jnp.exp(-x))


def _log_sigmoid(x):
    return jnp.minimum(x, 0.0) - jnp.log(1.0 + jnp.exp(-jnp.abs(x)))


def _split_bf16(x):
    hi = x.astype(BF16)
    lo = (x - hi.astype(F32)).astype(BF16)
    return hi, lo


def _norm_matmul_kernel(x_ref, g_ref, *refs):
    n = len(refs) // 2
    xn = _rms(x_ref[...], g_ref[...]).astype(BF16)
    for w_ref, o_ref in zip(refs[:n], refs[n:]):
        o_ref[...] = _dot(xn, w_ref[...]).astype(o_ref.dtype)


def _norm_matmul(x, g, weights, tm):
    rows, d = x.shape
    tm = min(tm, rows)
    in_specs = [pl.BlockSpec((tm, d), lambda i: (i, 0)), pl.BlockSpec((1, d), lambda i: (0, 0))]
    in_specs += [pl.BlockSpec(w.shape, lambda i: (0, 0)) for w in weights]
    out_specs = [pl.BlockSpec((tm, w.shape[1]), lambda i: (i, 0)) for w in weights]
    out_shape = [jax.ShapeDtypeStruct((rows, w.shape[1]), BF16) for w in weights]
    return pl.pallas_call(
        _norm_matmul_kernel, grid=(rows // tm,), in_specs=in_specs, out_specs=out_specs,
        out_shape=out_shape, compiler_params=_params("parallel"), name="norm_matmul",
    )(x, g.reshape(1, d), *weights)


def _gla_kernel(x_ref, og_ref, low_ref, cw_ref, gw_ref, gb_ref, ng_ref, o_ref, halo_ref, st_ref, *, chunk):
    c = chunk
    seq = x_ref.shape[0]
    qk_w = GLA_HEADS * V7X_LANES
    halo_ref[...] = jnp.zeros_like(halo_ref)
    st_ref[...] = jnp.zeros_like(st_ref)
    ri = lax.broadcasted_iota(jnp.int32, (c, c), 0)
    ci = lax.broadcasted_iota(jnp.int32, (c, c), 1)
    causal = ri >= ci
    lmat = jnp.where(causal, 1.0, 0.0).astype(BF16)
    cw = cw_ref[...]
    ng = ng_ref[...]

    def body(n, carry):
        r0 = pl.multiple_of(n * c, c)
        cur = x_ref[pl.ds(r0, c), :].astype(F32)
        ext = jnp.concatenate([halo_ref[...], cur], axis=0)
        y = cur * cw[CONV_WIDTH - 1:CONV_WIDTH, :]
        for j in range(1, CONV_WIDTH):
            tap = cw[CONV_WIDTH - 1 - j:CONV_WIDTH - j, :]
            y = y + pltpu.roll(ext, j, 0)[V7X_SUBLANES:, :] * tap
        halo_ref[...] = cur[c - V7X_SUBLANES:, :]
        y = y * _sigmoid(y)

        gpre = _dot(low_ref[pl.ds(r0, c), :], gw_ref[...]) + gb_ref[...]
        g = _log_sigmoid(gpre) * (1.0 / GLA_GATE_NORM)
        g_hi, g_lo = _split_bf16(g)
        cum = _dot(lmat, g_hi) + _dot(lmat, g_lo)

        for h in range(GLA_HEADS):
            sl = slice(h * V7X_LANES, (h + 1) * V7X_LANES)
            q = y[:, sl] * (GLA_DK ** -0.5)
            k = y[:, qk_w + h * V7X_LANES:qk_w + (h + 1) * V7X_LANES]
            v = y[:, 2 * qk_w + h * GLA_DV:2 * qk_w + (h + 1) * GLA_DV].astype(BF16)
            cm = cum[:, sl]
            last = cm[c - 1:c, :]
            mid = cm[c // 2 - 1:c // 2, :]
            q_in = (q * jnp.exp(cm)).astype(BF16)
            q_mid = (q * jnp.exp(cm - mid)).astype(BF16)
            k_mid = (k * jnp.exp(mid - cm)).astype(BF16)
            k_end = (k * jnp.exp(last - cm)).astype(BF16)
            scores = jnp.where(causal, _dot_nt(q_mid, k_mid), 0.0)
            state = st_ref[h]
            o = _dot(scores.astype(BF16), v) + _dot_nt(q_in, state.astype(BF16))
            st_ref[h] = state * jnp.exp(last) + _dot_tn(v, k_end)
            og = og_ref[pl.ds(r0, c), h * GLA_DV:(h + 1) * GLA_DV].astype(F32)
            o_ref[pl.ds(r0, c), h * GLA_DV:(h + 1) * GLA_DV] = (_rms(o, ng) * (og * _sigmoid(og))).astype(o_ref.dtype)
        return carry

    lax.fori_loop(0, seq // c, body, 0)


def _gla(conv_in, og, low, conv_w, gate_w, gate_b, norm_g, batch, seq):
    w = conv_in.shape[1]
    vw = og.shape[1]
    qk_w = GLA_HEADS * V7X_LANES
    kern = functools.partial(_gla_kernel, chunk=GLA_CHUNK)
    return pl.pallas_call(
        kern, grid=(batch,),
        in_specs=[
            pl.BlockSpec((seq, w), lambda b: (b, 0)),
            pl.BlockSpec((seq, vw), lambda b: (b, 0)),
            pl.BlockSpec((seq, V7X_LANES), lambda b: (b, 0)),
            pl.BlockSpec((CONV_WIDTH, w), lambda b: (0, 0)),
            pl.BlockSpec((V7X_LANES, qk_w), lambda b: (0, 0)),
            pl.BlockSpec((1, qk_w), lambda b: (0, 0)),
            pl.BlockSpec((1, GLA_DV), lambda b: (0, 0)),
        ],
        out_specs=pl.BlockSpec((seq, vw), lambda b: (b, 0)),
        out_shape=jax.ShapeDtypeStruct((batch * seq, vw), BF16),
        scratch_shapes=[pltpu.VMEM((V7X_SUBLANES, w), F32), pltpu.VMEM((GLA_HEADS, GLA_DV, V7X_LANES), F32)],
        compiler_params=_params("parallel"), name="gla",
    )(conv_in, og, low, conv_w, gate_w, gate_b, norm_g)


def _sb_kernel(q_ref, k_ref, v_ref, o_ref, acc_ref, carry_ref, *, blk):
    qi = pl.program_id(2)
    scale = SB_DH ** -0.5
    lane = lax.broadcasted_iota(jnp.int32, (1, V7X_LANES), 1)
    q2 = q_ref[...]
    zero = jnp.zeros_like(q2)
    q_heads = (jnp.where(lane < SB_DH, q2, zero), jnp.where(lane >= SB_DH, q2, zero))
    ri = lax.broadcasted_iota(jnp.int32, (blk, blk), 0)
    ci = lax.broadcasted_iota(jnp.int32, (blk, blk), 1)
    strict = ci < ri
    ri2 = lax.broadcasted_iota(jnp.int32, (blk, 2 * blk), 0)
    ci2 = lax.broadcasted_iota(jnp.int32, (blk, 2 * blk), 1)
    suffix_ones = jnp.where((ri2 > ci2) | (ci2 >= blk), 1.0, 0.0).astype(BF16)
    acc_ref[...] = jnp.zeros_like(acc_ref)
    carry_ref[...] = jnp.zeros_like(carry_ref)

    def block(kb, masked):
        k0 = pl.multiple_of(kb * blk, blk)
        kblk = k_ref[pl.ds(k0, blk), :]
        vblk = v_ref[pl.ds(k0, blk), :]
        for hh in range(2):
            z = _dot_nt(q_heads[hh], kblk) * scale
            soft = jnp.log(1.0 + jnp.exp(-jnp.abs(z)))
            log_beta = jnp.minimum(z, 0.0) - soft
            log_rest = log_beta - z
            if masked:
                log_rest = jnp.where(strict, log_rest, 0.0)
            hi, lo = _split_bf16(log_rest)
            sums = _dot(hi, suffix_ones) + _dot(lo, suffix_ones)
            carry = carry_ref[hh]
            w = jnp.exp(log_beta + sums[:, :blk] + carry)
            if masked:
                w = jnp.where(strict, w, 0.0)
            acc_ref[hh] += _dot(w.astype(BF16), vblk)
            carry_ref[hh] = carry + sums[:, blk:]

    block(qi, True)

    def body(step, c):
        block(qi - 1 - step, False)
        return c

    lax.fori_loop(0, qi, body, 0)
    o_ref[...] = jnp.where(lane < SB_DH, acc_ref[0], acc_ref[1]).astype(o_ref.dtype)


def _sb(q, k, v, batch, seq):
    blk = min(128, seq)
    nq = seq // blk
    pairs = SB_HEADS // 2
    kern = functools.partial(_sb_kernel, blk=blk)
    return pl.pallas_call(
        kern, grid=(batch, pairs, nq),
        in_specs=[
            pl.BlockSpec((blk, V7X_LANES), lambda b, g, i: (b * nq + i, g)),
            pl.BlockSpec((seq, V7X_LANES), lambda b, g, i: (b, g)),
            pl.BlockSpec((seq, V7X_LANES), lambda b, g, i: (b, g)),
        ],
        out_specs=pl.BlockSpec((blk, V7X_LANES), lambda b, g, i: (b * nq + i, g)),
        out_shape=jax.ShapeDtypeStruct(q.shape, BF16),
        scratch_shapes=[pltpu.VMEM((2, blk, V7X_LANES), F32), pltpu.VMEM((2, blk, V7X_LANES), F32)],
        compiler_params=_params("parallel", "parallel", "parallel"), name="stick_breaking",
    )(q, k, v)


def _mix_out_kernel(x_ref, og_ref, osb_ref, sbg_ref, wo_g_ref, wo_s_ref, xg_ref, wq_ref, h1_ref, q_ref):
    osb = _rms(osb_ref[...].astype(F32), sbg_ref[...]).astype(BF16)
    h1 = x_ref[...] + _dot(og_ref[...], wo_g_ref[...]) + _dot(osb, wo_s_ref[...])
    h1_ref[...] = h1
    q_ref[...] = _dot(_rms(h1, xg_ref[...]).astype(BF16), wq_ref[...]).astype(q_ref.dtype)


def _mix_out(x, o_gla, o_sb, sb_g, wo_g, wo_s, x_g, w_xq, tm):
    rows, d = x.shape
    tm = min(tm, rows)
    row = lambda w: pl.BlockSpec((tm, w), lambda i: (i, 0))
    full = lambda a: pl.BlockSpec(a.shape, lambda i: (0, 0))
    return pl.pallas_call(
        _mix_out_kernel, grid=(rows // tm,),
        in_specs=[row(d), row(o_gla.shape[1]), row(o_sb.shape[1]), full(sb_g), full(wo_g), full(wo_s), full(x_g), full(w_xq)],
        out_specs=[row(d), row(w_xq.shape[1])],
        out_shape=[jax.ShapeDtypeStruct((rows, d), F32), jax.ShapeDtypeStruct((rows, w_xq.shape[1]), BF16)],
        compiler_params=_params("parallel"), name="mix_out",
    )(x, o_gla, o_sb, sb_g, wo_g, wo_s, x_g, w_xq)


def _xattn_kernel(q_ref, kv_ref, h1_ref, wo_ref, fg_ref, wpq_ref, keys_ref, h2_ref, xn_ref, st_ref):
    xw = X_HEADS * X_DH
    outs = []
    for h in range(X_HEADS):
        qh = q_ref[:, h * X_DH:(h + 1) * X_DH]
        kh = kv_ref[:, h * X_DH:(h + 1) * X_DH]
        vh = kv_ref[:, xw + h * X_DH:xw + (h + 1) * X_DH]
        s = _dot_nt(qh, kh) * (X_DH ** -0.5)
        p = jnp.exp(s - jnp.max(s, axis=-1, keepdims=True))
        o = _dot(p.astype(BF16), vh) / jnp.sum(p, axis=-1, keepdims=True)
        outs.append(o.astype(BF16))
    h2 = h1_ref[...] + _dot(jnp.concatenate(outs, axis=-1), wo_ref[...])
    h2_ref[...] = h2
    xn = _rms(h2, fg_ref[...]).astype(BF16)
    xn_ref[...] = xn
    pq = _dot(xn, wpq_ref[...])
    half = keys_ref.shape[2]
    for hp in range(keys_ref.shape[0]):
        st_ref[hp] = _dot_nt(keys_ref[hp], pq[:, hp * half:(hp + 1) * half].astype(BF16))


def _xattn(q, kv, h1, w_xo, f_g, w_pq, keys, batch, seq, tm):
    rows, d = h1.shape
    tm = min(tm, seq)
    nt = seq // tm
    mem_len = kv.shape[0] // batch
    row = lambda w: pl.BlockSpec((tm, w), lambda b, i: (b * nt + i, 0))
    full = lambda a: pl.BlockSpec(a.shape, lambda b, i: (0,) * a.ndim)
    n_hp, n_keys, _ = keys.shape
    return pl.pallas_call(
        _xattn_kernel, grid=(batch, nt),
        in_specs=[row(q.shape[1]), pl.BlockSpec((mem_len, kv.shape[1]), lambda b, i: (b, 0)), row(d),
                  full(w_xo), full(f_g), full(w_pq), full(keys)],
        out_specs=[row(d), row(d), pl.BlockSpec((n_hp, n_keys, tm), lambda b, i: (0, 0, b * nt + i))],
        out_shape=[jax.ShapeDtypeStruct((rows, d), F32), jax.ShapeDtypeStruct((rows, d), BF16),
                   jax.ShapeDtypeStruct((n_hp, n_keys, rows), F32)],
        compiler_params=_params("parallel", "parallel"), name="xattn_peer_scores",
    )(q, kv, h1, w_xo, f_g, w_pq, keys)


def _peer_sel_kernel(st_ref, d_ref, e1_ref, e2_ref):
    n_top = PEER_TOPK + 1

    def top_values(x):
        vals = []
        for _ in range(n_top):
            m = jnp.max(x, axis=0, keepdims=True)
            vals.append(m)
            x = jnp.where(x == m, NEG_INF, x)
        return vals

    tops = [[top_values(st_ref[h, p]) for h in range(PEER_HEADS)] for p in range(2)]
    packed = [[jnp.concatenate([tops[p][h][r] for h in range(PEER_HEADS)], axis=0) for r in range(n_top)]
              for p in range(2)]
    cands = [packed[0][i] + packed[1][j] for i in range(n_top) for j in range(n_top) if (i + 1) * (j + 1) <= n_top]
    best = []
    for _ in range(n_top):
        m = functools.reduce(jnp.maximum, cands)
        best.append(m)
        cands = [jnp.where(c == m, NEG_INF, c) for c in cands]
    thr = 0.5 * (best[PEER_TOPK - 1] + best[PEER_TOPK])
    z = functools.reduce(lambda a, b: a + b, [jnp.exp(b - best[0]) for b in best[:PEER_TOPK]])
    inv_z = 1.0 / z
    for h in range(PEER_HEADS):
        s1 = st_ref[h, 0]
        s2 = st_ref[h, 1]
        d_ref[h] = thr[h:h + 1, :] - s1
        e1_ref[h] = jnp.exp(s1 - tops[0][h][0]) * inv_z[h:h + 1, :]
        e2_ref[h] = jnp.exp(s2 - tops[1][h][0])


def _peer_sel(st, tt):
    heads, _, n_keys, rows = st.shape
    tt = min(tt, rows)
    spec = pl.BlockSpec((heads, n_keys, tt), lambda i: (0, 0, i))
    shape = jax.ShapeDtypeStruct((heads, n_keys, rows), F32)
    return pl.pallas_call(
        _peer_sel_kernel, grid=(rows // tt,),
        in_specs=[pl.BlockSpec((heads, 2, n_keys, tt), lambda i: (0, 0, 0, i))],
        out_specs=[spec, spec, spec], out_shape=[shape, shape, shape],
        compiler_params=_params("parallel"), name="peer_select",
    )(st)


def _gelu_tanh(x):
    return 0.5 * x * (1.0 + jnp.tanh(0.7978845608028654 * (x + 0.044715 * (x * x * x))))


def _peer_ffn_kernel(xn_ref, u_ref, vt_ref, d_ref, s2_ref, e1_ref, e2_ref, h2_ref, fg_ref, o_ref,
                     acc_ref, act_ref, gate_ref, *, tile, final_norm):
    e = pl.program_id(1)
    ec, tt = act_ref.shape
    nk = PEER_NKEYS
    chunks = ec // nk

    @pl.when(e == 0)
    def _():
        acc_ref[...] = jnp.zeros_like(acc_ref)

    act_ref[...] = _dot_nt(u_ref[...], xn_ref[...])

    assert chunks == V7X_SUBLANES
    for c in range(chunks):
        def body(it, carry, c=c):
            t0 = pl.multiple_of(it * tile, tile)
            w = jnp.zeros((nk, tile), F32)
            for h in range(PEER_HEADS):
                d_i = d_ref[h, e, c:c + 1, pl.ds(t0, tile)]
                e1_i = e1_ref[h, e, c:c + 1, pl.ds(t0, tile)]
                sel = s2_ref[h, :, pl.ds(t0, tile)] >= d_i
                w = w + jnp.where(sel, e2_ref[h, :, pl.ds(t0, tile)], 0.0) * e1_i
            a = act_ref[c * nk:(c + 1) * nk, pl.ds(t0, tile)]
            gate_ref[c * nk:(c + 1) * nk, pl.ds(t0, tile)] = (w * _gelu_tanh(a)).astype(gate_ref.dtype)
            return carry

        lax.fori_loop(0, tt // tile, body, 0)
    acc_ref[...] += _dot(vt_ref[...], gate_ref[...])

    @pl.when(e == pl.num_programs(1) - 1)
    def _():
        out = h2_ref[...] + acc_ref[...].T
        o_ref[...] = _rms(out, fg_ref[...]) if final_norm else out


def _peer_ffn(xn, u, vt, d, st, e1, e2, h2, f_g, final_norm, tt, ec):
    rows, dm = xn.shape
    n_exp = u.shape[0]
    tt = min(tt, rows)
    heads, n_keys, _ = d.shape
    tile = min(V7X_LANES, tt)
    sel_spec = pl.BlockSpec((heads, n_keys, tt), lambda i, e: (0, 0, i))
    grp_spec = pl.BlockSpec((heads, n_keys // V7X_SUBLANES, V7X_SUBLANES, tt), lambda i, e: (0, 0, 0, i))
    d = d.reshape(heads, n_keys // V7X_SUBLANES, V7X_SUBLANES, rows)
    e1 = e1.reshape(heads, n_keys // V7X_SUBLANES, V7X_SUBLANES, rows)
    kern = functools.partial(_peer_ffn_kernel, tile=tile, final_norm=final_norm)
    return pl.pallas_call(
        kern, grid=(rows // tt, n_exp // ec),
        in_specs=[
            pl.BlockSpec((tt, dm), lambda i, e: (i, 0)),
            pl.BlockSpec((ec, dm), lambda i, e: (e, 0)),
            pl.BlockSpec((dm, ec), lambda i, e: (0, e)),
            grp_spec,
            pl.BlockSpec((heads, None, n_keys, tt), lambda i, e: (0, 1, 0, i)),
            grp_spec, sel_spec,
            pl.BlockSpec((tt, dm), lambda i, e: (i, 0)),
            pl.BlockSpec((1, dm), lambda i, e: (0, 0)),
        ],
        out_specs=pl.BlockSpec((tt, dm), lambda i, e: (i, 0)),
        out_shape=jax.ShapeDtypeStruct((rows, dm), F32),
        scratch_shapes=[pltpu.VMEM((dm, tt), F32), pltpu.VMEM((ec, tt), F32), pltpu.VMEM((ec, tt), BF16)],
        compiler_params=_params("parallel", "arbitrary"), name="peer_ffn",
    )(xn, u, vt, d, st, e1, e2, h2, f_g)


def _pad_head_cols(w, heads, width):
    lead = w.shape[:-1]
    w = w.reshape(lead + (heads, width))
    w = jnp.pad(w, [(0, 0)] * len(lead) + [(0, 0), (0, V7X_LANES - width)])
    return w.reshape(lead + (heads * V7X_LANES,))


def kernel(x, mem, mix_norm_g, w_in, gla_conv_w, gla_gate_w, gla_gate_b, gla_out_norm_g, sb_out_norm_g, w_out,
           xattn_norm_g, mem_norm_g, w_xq, w_xkv, w_xo, ffn_norm_g, peer_w_q, peer_sub_keys, peer_u, peer_v,
           final_norm_g):
    batch, seq, d = x.shape
    mem_len = mem.shape[1]
    rows = batch * seq
    depth = w_in.shape[0]
    qk = GLA_HEADS * GLA_DK
    vw = GLA_HEADS * GLA_DV
    sbw = SB_HEADS * SB_DH
    c1 = 2 * qk + vw
    c2 = c1 + vw
    c3 = c2 + GLA_LOWRANK
    c4 = c3 + sbw
    c5 = c4 + sbw

    h = x.reshape(rows, d)
    mem2 = mem.reshape(batch * mem_len, d)
    for l in range(depth):
        wl = w_in[l]
        w_conv = jnp.concatenate([_pad_head_cols(wl[:, :qk], GLA_HEADS, GLA_DK),
                                  _pad_head_cols(wl[:, qk:2 * qk], GLA_HEADS, GLA_DK),
                                  wl[:, 2 * qk:c1]], axis=1).astype(BF16)
        w_og = wl[:, c1:c2].astype(BF16)
        w_low = jnp.pad(wl[:, c2:c3], ((0, 0), (0, V7X_LANES - GLA_LOWRANK))).astype(BF16)
        w_sbq, w_sbk, w_sbv = (wl[:, a:b].astype(BF16) for a, b in ((c3, c4), (c4, c5), (c5, c5 + sbw)))
        cw = gla_conv_w[l]
        conv_w = jnp.concatenate([_pad_head_cols(cw[:, :qk], GLA_HEADS, GLA_DK),
                                  _pad_head_cols(cw[:, qk:2 * qk], GLA_HEADS, GLA_DK), cw[:, 2 * qk:]], axis=1)
        gate_w = jnp.pad(_pad_head_cols(gla_gate_w[l], GLA_HEADS, GLA_DK),
                         ((0, V7X_LANES - GLA_LOWRANK), (0, 0))).astype(BF16)
        gate_b = _pad_head_cols(gla_gate_b[l], GLA_HEADS, GLA_DK).reshape(1, -1)

        conv_in, og, low, sbq, sbk, sbv = _norm_matmul(
            h, mix_norm_g[l], [w_conv, w_og, w_low, w_sbq, w_sbk, w_sbv], tm=512)
        o_gla = _gla(conv_in, og, low, conv_w, gate_w, gate_b, gla_out_norm_g[l].reshape(1, -1), batch, seq)
        o_sb = _sb(sbq, sbk, sbv, batch, seq)

        wo = w_out[l].astype(BF16)
        h1, xq = _mix_out(h, o_gla, o_sb, sb_out_norm_g[l].reshape(1, -1), wo[:vw], wo[vw:],
                          xattn_norm_g[l].reshape(1, -1), w_xq[l].astype(BF16), tm=512)
        (kv,) = _norm_matmul(mem2, mem_norm_g[l], [w_xkv[l].astype(BF16)], tm=512)
        keys = peer_sub_keys[l].reshape(PEER_HEADS * 2, PEER_NKEYS, -1).astype(BF16)
        h2, xn, st = _xattn(xq, kv, h1, w_xo[l].astype(BF16), ffn_norm_g[l].reshape(1, -1),
                            peer_w_q[l].astype(BF16), keys, batch, seq, tm=512)
        st = st.reshape(PEER_HEADS, 2, PEER_NKEYS, rows)
        dthr, e1, e2 = _peer_sel(st, tt=256)
        h = _peer_ffn(xn, peer_u[l].astype(BF16), peer_v[l].T.astype(BF16), dthr, st, e1, e2, h2,
                      final_norm_g.reshape(1, -1), final_norm=(l == depth - 1), tt=512, ec=V7X_SUBLANES * PEER_NKEYS)
    return h.reshape(batch, seq, d)
```

```python
import functools

import jax
import jax.numpy as jnp
from jax import lax
from jax.experimental import pallas as pl
from jax.experimental.pallas import tpu as pltpu

EPS = 1e-6
GLA_HEADS = 4
GLA_DK = 64
GLA_DV = 128
GLA_LOWRANK = 16
GLA_GATE_NORM = 16.0
GLA_CHUNK = 64
CONV_WIDTH = 4
SB_HEADS = 8
SB_DH = 64
SB_BLOCK = 256
X_HEADS = 4
X_DH = 256
PEER_HEADS = 8
PEER_NKEYS = 128
PEER_TOPK = 16

V7X_LANES = 128
V7X_SUBLANES = 8
V7X_VMEM_LIMIT_BYTES = 52 * 1024 * 1024

F32 = jnp.float32
BF16 = jnp.bfloat16
NEG_INF = float("-inf")


def _params(*semantics):
    return pltpu.CompilerParams(dimension_semantics=semantics, vmem_limit_bytes=V7X_VMEM_LIMIT_BYTES)


def _dot(a, b):
    return jnp.dot(a, b, preferred_element_type=F32)


def _dot_nt(a, b):
    return lax.dot_general(a, b, (((1,), (1,)), ((), ())), preferred_element_type=F32)


def _dot_tn(a, b):
    return lax.dot_general(a, b, (((0,), (0,)), ((), ())), preferred_element_type=F32)


def _rms(x, g):
    return x * lax.rsqrt(jnp.mean(x * x, axis=-1, keepdims=True) + EPS) * g


def _sigmoid(x):
    return 1.0 / (1.0 + jnp.exp(-x))


def _log_sigmoid(x):
    return jnp.minimum(x, 0.0) - jnp.log(1.0 + jnp.exp(-jnp.abs(x)))


def _split_bf16(x):
    hi = x.astype(BF16)
    lo = (x - hi.astype(F32)).astype(BF16)
    return hi, lo


def _norm_matmul_kernel(x_ref, g_ref, *refs):
    n = len(refs) // 2
    xn = _rms(x_ref[...], g_ref[...]).astype(BF16)
    for w_ref, o_ref in zip(refs[:n], refs[n:]):
        o_ref[...] = _dot(xn, w_ref[...]).astype(o_ref.dtype)


def _norm_matmul(x, g, weights, tm):
    rows, d = x.shape
    tm = min(tm, rows)
    in_specs = [pl.BlockSpec((tm, d), lambda i: (i, 0)), pl.BlockSpec((1, d), lambda i: (0, 0))]
    in_specs += [pl.BlockSpec(w.shape, lambda i: (0, 0)) for w in weights]
    out_specs = [pl.BlockSpec((tm, w.shape[1]), lambda i: (i, 0)) for w in weights]
    out_shape = [jax.ShapeDtypeStruct((rows, w.shape[1]), BF16) for w in weights]
    return pl.pallas_call(
        _norm_matmul_kernel, grid=(rows // tm,), in_specs=in_specs, out_specs=out_specs,
        out_shape=out_shape, compiler_params=_params("parallel"), name="norm_matmul",
    )(x, g.reshape(1, d), *weights)


def _gla_kernel(x_ref, og_ref, low_ref, cw_ref, gw_ref, gb_ref, ng_ref, o_ref, halo_ref, st_ref, *, chunk):
    c = chunk
    seq = x_ref.shape[0]
    qk_w = GLA_HEADS * V7X_LANES
    halo_ref[...] = jnp.zeros_like(halo_ref)
    st_ref[...] = jnp.zeros_like(st_ref)
    ri = lax.broadcasted_iota(jnp.int32, (c, c), 0)
    ci = lax.broadcasted_iota(jnp.int32, (c, c), 1)
    causal = ri >= ci
    lmat = jnp.where(causal, 1.0, 0.0).astype(BF16)
    cw = cw_ref[...]
    ng = ng_ref[...]

    def body(n, carry):
        r0 = pl.multiple_of(n * c, c)
        cur = x_ref[pl.ds(r0, c), :].astype(F32)
        ext = jnp.concatenate([halo_ref[...], cur], axis=0)
        y = cur * cw[CONV_WIDTH - 1:CONV_WIDTH, :]
        for j in range(1, CONV_WIDTH):
            tap = cw[CONV_WIDTH - 1 - j:CONV_WIDTH - j, :]
            y = y + pltpu.roll(ext, j, 0)[V7X_SUBLANES:, :] * tap
        halo_ref[...] = cur[c - V7X_SUBLANES:, :]
        y = y * _sigmoid(y)

        gpre = _dot(low_ref[pl.ds(r0, c), :], gw_ref[...]) + gb_ref[...]
        g = _log_sigmoid(gpre) * (1.0 / GLA_GATE_NORM)
        g_hi, g_lo = _split_bf16(g)
        cum = _dot(lmat, g_hi) + _dot(lmat, g_lo)

        for h in range(GLA_HEADS):
            sl = slice(h * V7X_LANES, (h + 1) * V7X_LANES)
            q = y[:, sl] * (GLA_DK ** -0.5)
            k = y[:, qk_w + h * V7X_LANES:qk_w + (h + 1) * V7X_LANES]
            v = y[:, 2 * qk_w + h * GLA_DV:2 * qk_w + (h + 1) * GLA_DV].astype(BF16)
            cm = cum[:, sl]
            last = cm[c - 1:c, :]
            mid = cm[c // 2 - 1:c // 2, :]
            q_in = (q * jnp.exp(cm)).astype(BF16)
            q_mid = (q * jnp.exp(cm - mid)).astype(BF16)
            k_mid = (k * jnp.exp(mid - cm)).astype(BF16)
            k_end = (k * jnp.exp(last - cm)).astype(BF16)
            scores = jnp.where(causal, _dot_nt(q_mid, k_mid), 0.0)
            state = st_ref[h]
            o = _dot(scores.astype(BF16), v) + _dot_nt(q_in, state.astype(BF16))
            st_ref[h] = state * jnp.exp(last) + _dot_tn(v, k_end)
            og = og_ref[pl.ds(r0, c), h * GLA_DV:(h + 1) * GLA_DV].astype(F32)
            o_ref[pl.ds(r0, c), h * GLA_DV:(h + 1) * GLA_DV] = (_rms(o, ng) * (og * _sigmoid(og))).astype(o_ref.dtype)
        return carry

    lax.fori_loop(0, seq // c, body, 0)


def _gla(conv_in, og, low, conv_w, gate_w, gate_b, norm_g, batch, seq):
    w = conv_in.shape[1]
    vw = og.shape[1]
    qk_w = GLA_HEADS * V7X_LANES
    kern = functools.partial(_gla_kernel, chunk=GLA_CHUNK)
    return pl.pallas_call(
        kern, grid=(batch,),
        in_specs=[
            pl.BlockSpec((seq, w), lambda b: (b, 0)),
            pl.BlockSpec((seq, vw), lambda b: (b, 0)),
            pl.BlockSpec((seq, V7X_LANES), lambda b: (b, 0)),
            pl.BlockSpec((CONV_WIDTH, w), lambda b: (0, 0)),
            pl.BlockSpec((V7X_LANES, qk_w), lambda b: (0, 0)),
            pl.BlockSpec((1, qk_w), lambda b: (0, 0)),
            pl.BlockSpec((1, GLA_DV), lambda b: (0, 0)),
        ],
        out_specs=pl.BlockSpec((seq, vw), lambda b: (b, 0)),
        out_shape=jax.ShapeDtypeStruct((batch * seq, vw), BF16),
        scratch_shapes=[pltpu.VMEM((V7X_SUBLANES, w), F32), pltpu.VMEM((GLA_HEADS, GLA_DV, V7X_LANES), F32)],
        compiler_params=_params("parallel"), name="gla",
    )(conv_in, og, low, conv_w, gate_w, gate_b, norm_g)


def _sb_kernel(q_ref, k_ref, v_ref, o_ref, acc_ref, carry_ref, *, blk):
    qi = pl.program_id(2)
    scale = SB_DH ** -0.5
    lane = lax.broadcasted_iota(jnp.int32, (1, V7X_LANES), 1)
    q2 = q_ref[...]
    zero = jnp.zeros_like(q2)
    q_heads = (jnp.where(lane < SB_DH, q2, zero), jnp.where(lane >= SB_DH, q2, zero))
    ri = lax.broadcasted_iota(jnp.int32, (blk, blk), 0)
    ci = lax.broadcasted_iota(jnp.int32, (blk, blk), 1)
    strict = ci < ri
    ri2 = lax.broadcasted_iota(jnp.int32, (blk, 2 * blk), 0)
    ci2 = lax.broadcasted_iota(jnp.int32, (blk, 2 * blk), 1)
    suffix_ones = jnp.where((ri2 > ci2) | (ci2 >= blk), 1.0, 0.0).astype(BF16)
    acc_ref[...] = jnp.zeros_like(acc_ref)
    carry_ref[...] = jnp.zeros_like(carry_ref)

    def block(kb, masked):
        k0 = pl.multiple_of(kb * blk, blk)
        kblk = k_ref[pl.ds(k0, blk), :]
        vblk = v_ref[pl.ds(k0, blk), :]
        for hh in range(2):
            z = _dot_nt(q_heads[hh], kblk) * scale
            soft = jnp.log(1.0 + jnp.exp(-jnp.abs(z)))
            log_beta = jnp.minimum(z, 0.0) - soft
            log_rest = log_beta - z
            if masked:
                log_rest = jnp.where(strict, log_rest, 0.0)
            hi, lo = _split_bf16(log_rest)
            sums = _dot(hi, suffix_ones) + _dot(lo, suffix_ones)
            carry = carry_ref[hh]
            w = jnp.exp(log_beta + sums[:, :blk] + carry)
            if masked:
                w = jnp.where(strict, w, 0.0)
            acc_ref[hh] += _dot(w.astype(BF16), vblk)
            carry_ref[hh] = carry + sums[:, blk:]

    block(qi, True)

    def body(step, c):
        block(qi - 1 - step, False)
        return c

    lax.fori_loop(0, qi, body, 0)
    o_ref[...] = jnp.where(lane < SB_DH, acc_ref[0], acc_ref[1]).astype(o_ref.dtype)


def _sb(q, k, v, batch, seq):
    blk = min(SB_BLOCK, seq)
    nq = seq // blk
    pairs = SB_HEADS // 2
    kern = functools.partial(_sb_kernel, blk=blk)
    return pl.pallas_call(
        kern, grid=(batch, pairs, nq),
        in_specs=[
            pl.BlockSpec((blk, V7X_LANES), lambda b, g, i: (b * nq + i, g)),
            pl.BlockSpec((seq, V7X_LANES), lambda b, g, i: (b, g)),
            pl.BlockSpec((seq, V7X_LANES), lambda b, g, i: (b, g)),
        ],
        out_specs=pl.BlockSpec((blk, V7X_LANES), lambda b, g, i: (b * nq + i, g)),
        out_shape=jax.ShapeDtypeStruct(q.shape, BF16),
        scratch_shapes=[pltpu.VMEM((2, blk, V7X_LANES), F32), pltpu.VMEM((2, blk, blk), F32)],
        compiler_params=_params("parallel", "parallel", "parallel"), name="stick_breaking",
    )(q, k, v)


def _mix_out_kernel(x_ref, og_ref, osb_ref, sbg_ref, wo_g_ref, wo_s_ref, xg_ref, wq_ref, h1_ref, q_ref):
    osb = _rms(osb_ref[...].astype(F32), sbg_ref[...]).astype(BF16)
    h1 = x_ref[...] + _dot(og_ref[...], wo_g_ref[...]) + _dot(osb, wo_s_ref[...])
    h1_ref[...] = h1
    q_ref[...] = _dot(_rms(h1, xg_ref[...]).astype(BF16), wq_ref[...]).astype(q_ref.dtype)


def _mix_out(x, o_gla, o_sb, sb_g, wo_g, wo_s, x_g, w_xq, tm):
    rows, d = x.shape
    tm = min(tm, rows)
    row = lambda w: pl.BlockSpec((tm, w), lambda i: (i, 0))
    full = lambda a: pl.BlockSpec(a.shape, lambda i: (0, 0))
    return pl.pallas_call(
        _mix_out_kernel, grid=(rows // tm,),
        in_specs=[row(d), row(o_gla.shape[1]), row(o_sb.shape[1]), full(sb_g), full(wo_g), full(wo_s), full(x_g), full(w_xq)],
        out_specs=[row(d), row(w_xq.shape[1])],
        out_shape=[jax.ShapeDtypeStruct((rows, d), F32), jax.ShapeDtypeStruct((rows, w_xq.shape[1]), BF16)],
        compiler_params=_params("parallel"), name="mix_out",
    )(x, o_gla, o_sb, sb_g, wo_g, wo_s, x_g, w_xq)


def _xattn_kernel(q_ref, kv_ref, h1_ref, wo_ref, fg_ref, wpq_ref, keys_ref, h2_ref, xn_ref, st_ref):
    xw = X_HEADS * X_DH
    outs = []
    for h in range(X_HEADS):
        qh = q_ref[:, h * X_DH:(h + 1) * X_DH]
        kh = kv_ref[:, h * X_DH:(h + 1) * X_DH]
        vh = kv_ref[:, xw + h * X_DH:xw + (h + 1) * X_DH]
        s = _dot_nt(qh, kh) * (X_DH ** -0.5)
        p = jnp.exp(s - jnp.max(s, axis=-1, keepdims=True))
        o = _dot(p.astype(BF16), vh) / jnp.sum(p, axis=-1, keepdims=True)
        outs.append(o.astype(BF16))
    h2 = h1_ref[...] + _dot(jnp.concatenate(outs, axis=-1), wo_ref[...])
    h2_ref[...] = h2
    xn = _rms(h2, fg_ref[...]).astype(BF16)
    xn_ref[...] = xn
    pq = _dot(xn, wpq_ref[...])
    half = keys_ref.shape[2]
    for hp in range(keys_ref.shape[0]):
        st_ref[hp] = _dot_nt(keys_ref[hp], pq[:, hp * half:(hp + 1) * half].astype(BF16))


def _xattn(q, kv, h1, w_xo, f_g, w_pq, keys, batch, seq, tm):
    rows, d = h1.shape
    tm = min(tm, seq)
    nt = seq // tm
    mem_len = kv.shape[0] // batch
    row = lambda w: pl.BlockSpec((tm, w), lambda b, i: (b * nt + i, 0))
    full = lambda a: pl.BlockSpec(a.shape, lambda b, i: (0,) * a.ndim)
    n_hp, n_keys, _ = keys.shape
    return pl.pallas_call(
        _xattn_kernel, grid=(batch, nt),
        in_specs=[row(q.shape[1]), pl.BlockSpec((mem_len, kv.shape[1]), lambda b, i: (b, 0)), row(d),
                  full(w_xo), full(f_g), full(w_pq), full(keys)],
        out_specs=[row(d), row(d), pl.BlockSpec((n_hp, n_keys, tm), lambda b, i: (0, 0, b * nt + i))],
        out_shape=[jax.ShapeDtypeStruct((rows, d), F32), jax.ShapeDtypeStruct((rows, d), BF16),
                   jax.ShapeDtypeStruct((n_hp, n_keys, rows), F32)],
        compiler_params=_params("parallel", "parallel"), name="xattn_peer_scores",
    )(q, kv, h1, w_xo, f_g, w_pq, keys)


def _peer_sel_kernel(st_ref, cnt_ref, e1_ref, rank2_ref, e2_ref):
    n_top = PEER_TOPK + 1
    not_ranked = float(PEER_NKEYS - 1)

    def top_values(x):
        vals = []
        rank = jnp.full(x.shape, not_ranked, F32)
        for r in range(n_top):
            m = jnp.max(x, axis=0, keepdims=True)
            hit = x == m
            vals.append(m)
            rank = jnp.where(hit, float(r), rank)
            x = jnp.where(hit, NEG_INF, x)
        return vals, rank

    tops = [[top_values(st_ref[h, p]) for h in range(PEER_HEADS)] for p in range(2)]
    packed = [[jnp.concatenate([tops[p][h][0][r] for h in range(PEER_HEADS)], axis=0) for r in range(n_top)]
              for p in range(2)]
    pairs = [(i, j) for i in range(n_top) for j in range(n_top) if (i + 1) * (j + 1) <= n_top]
    sums = {ij: packed[0][ij[0]] + packed[1][ij[1]] for ij in pairs}
    cands = [sums[ij] for ij in pairs]
    best = []
    for _ in range(n_top):
        m = functools.reduce(jnp.maximum, cands)
        best.append(m)
        cands = [jnp.where(c == m, NEG_INF, c) for c in cands]
    thr = 0.5 * (best[PEER_TOPK - 1] + best[PEER_TOPK])
    z = functools.reduce(lambda a, b: a + b, [jnp.exp(b - best[0]) for b in best[:PEER_TOPK]])
    inv_z = 1.0 / z
    cnt_by_rank = [functools.reduce(lambda a, b: a + b,
                                    [jnp.where(sums[(i, j)] >= thr, 1.0, 0.0) for j in range(n_top) if (i, j) in sums])
                   for i in range(PEER_TOPK)]
    for h in range(PEER_HEADS):
        rank1 = tops[0][h][1]
        cnt = jnp.zeros(rank1.shape, F32)
        for i in range(PEER_TOPK):
            cnt = jnp.where(rank1 == float(i), cnt_by_rank[i][h:h + 1, :], cnt)
        cnt_ref[h] = cnt
        e1_ref[h] = jnp.exp(st_ref[h, 0] - tops[0][h][0][0]) * inv_z[h:h + 1, :]
        rank2_ref[h] = tops[1][h][1].astype(rank2_ref.dtype)
        e2_ref[h] = jnp.exp(st_ref[h, 1] - tops[1][h][0][0]).astype(e2_ref.dtype)


def _peer_sel(st, tt):
    heads, _, n_keys, rows = st.shape
    tt = min(tt, rows)
    spec = pl.BlockSpec((heads, n_keys, tt), lambda i: (0, 0, i))
    f32_shape = jax.ShapeDtypeStruct((heads, n_keys, rows), F32)
    bf16_shape = jax.ShapeDtypeStruct((heads, n_keys, rows), BF16)
    return pl.pallas_call(
        _peer_sel_kernel, grid=(rows // tt,),
        in_specs=[pl.BlockSpec((heads, 2, n_keys, tt), lambda i: (0, 0, 0, i))],
        out_specs=[spec, spec, spec, spec], out_shape=[f32_shape, f32_shape, bf16_shape, bf16_shape],
        compiler_params=_params("parallel"), name="peer_select",
    )(st)


def _gelu_tanh(x):
    return 0.5 * x * (1.0 + jnp.tanh(0.7978845608028654 * (x + 0.044715 * (x * x * x))))


def _peer_ffn_kernel(xn_ref, u_ref, vt_ref, cnt_ref, e1_ref, rank2_ref, e2_ref, h2_ref, fg_ref, o_ref,
                     acc_ref, act_ref, gate_ref, *, final_norm):
    s = pl.program_id(1)
    n_groups = pl.num_programs(1) - 2
    _, ec, tt = act_ref.shape
    nk = PEER_NKEYS
    pack = 2 * V7X_SUBLANES
    assert ec == V7X_SUBLANES * nk

    @pl.when(s == 0)
    def _():
        acc_ref[...] = jnp.zeros_like(acc_ref)
        act_ref[...] = jnp.zeros_like(act_ref)
        gate_ref[...] = jnp.zeros_like(gate_ref)

    cur = s % 2
    prev = 1 - cur
    act_ref[cur] = _dot_nt(u_ref[...], xn_ref[...])

    grp = jnp.clip(s - 1, 0, n_groups - 1)
    for c in range(V7X_SUBLANES):
        for tl in range(tt // V7X_LANES):
            ts = slice(tl * V7X_LANES, (tl + 1) * V7X_LANES)
            w = [None] * (nk // pack)
            for h in range(PEER_HEADS):
                cnt_b = jnp.broadcast_to(cnt_ref[h, grp, c:c + 1, ts], (pack, V7X_LANES)).astype(BF16)
                e1_b = jnp.broadcast_to(e1_ref[h, grp, c:c + 1, ts], (pack, V7X_LANES)).astype(BF16)
                for rg in range(nk // pack):
                    rs = slice(rg * pack, (rg + 1) * pack)
                    e2 = e2_ref[h, rs, ts]
                    term = jnp.where(rank2_ref[h, rs, ts] < cnt_b, e2, jnp.zeros_like(e2)) * e1_b
                    w[rg] = term if w[rg] is None else w[rg] + term
            for rg in range(nk // pack):
                rows = slice(c * nk + rg * pack, c * nk + (rg + 1) * pack)
                gate_ref[prev, rows, ts] = _gelu_tanh(act_ref[prev, rows, ts]).astype(BF16) * w[rg]

    acc_ref[...] += _dot(vt_ref[...], gate_ref[cur])

    @pl.when(s == pl.num_programs(1) - 1)
    def _():
        out = h2_ref[...] + acc_ref[...].T
        o_ref[...] = _rms(out, fg_ref[...]) if final_norm else out


def _peer_ffn(xn, u, vt, cnt, e1, rank2, e2, h2, f_g, final_norm, tt):
    rows, dm = xn.shape
    tt = min(tt, rows)
    heads, n_keys, _ = cnt.shape
    ec = V7X_SUBLANES * n_keys
    n_groups = u.shape[0] // ec
    sel_spec = pl.BlockSpec((heads, n_keys, tt), lambda i, s: (0, 0, i))
    grp_spec = pl.BlockSpec((heads, n_keys // V7X_SUBLANES, V7X_SUBLANES, tt), lambda i, s: (0, 0, 0, i))
    cnt = cnt.reshape(heads, n_keys // V7X_SUBLANES, V7X_SUBLANES, rows)
    e1 = e1.reshape(heads, n_keys // V7X_SUBLANES, V7X_SUBLANES, rows)
    kern = functools.partial(_peer_ffn_kernel, final_norm=final_norm)
    return pl.pallas_call(
        kern, grid=(rows // tt, n_groups + 2),
        in_specs=[
            pl.BlockSpec((tt, dm), lambda i, s: (i, 0)),
            pl.BlockSpec((ec, dm), lambda i, s: (jnp.minimum(s, n_groups - 1), 0)),
            pl.BlockSpec((dm, ec), lambda i, s: (0, jnp.clip(s - 2, 0, n_groups - 1))),
            grp_spec, grp_spec, sel_spec, sel_spec,
            pl.BlockSpec((tt, dm), lambda i, s: (i, 0)),
            pl.BlockSpec((1, dm), lambda i, s: (0, 0)),
        ],
        out_specs=pl.BlockSpec((tt, dm), lambda i, s: (i, 0)),
        out_shape=jax.ShapeDtypeStruct((rows, dm), F32),
        scratch_shapes=[pltpu.VMEM((dm, tt), F32), pltpu.VMEM((2, ec, tt), F32), pltpu.VMEM((2, ec, tt), BF16)],
        compiler_params=_params("parallel", "arbitrary"), name="peer_ffn",
    )(xn, u, vt, cnt, e1, rank2, e2, h2, f_g)


def _pad_head_cols(w, heads, width):
    lead = w.shape[:-1]
    w = w.reshape(lead + (heads, width))
    w = jnp.pad(w, [(0, 0)] * len(lead) + [(0, 0), (0, V7X_LANES - width)])
    return w.reshape(lead + (heads * V7X_LANES,))


def kernel(x, mem, mix_norm_g, w_in, gla_conv_w, gla_gate_w, gla_gate_b, gla_out_norm_g, sb_out_norm_g, w_out,
           xattn_norm_g, mem_norm_g, w_xq, w_xkv, w_xo, ffn_norm_g, peer_w_q, peer_sub_keys, peer_u, peer_v,
           final_norm_g):
    batch, seq, d = x.shape
    mem_len = mem.shape[1]
    rows = batch * seq
    depth = w_in.shape[0]
    qk = GLA_HEADS * GLA_DK
    vw = GLA_HEADS * GLA_DV
    sbw = SB_HEADS * SB_DH
    c1 = 2 * qk + vw
    c2 = c1 + vw
    c3 = c2 + GLA_LOWRANK
    c4 = c3 + sbw
    c5 = c4 + sbw

    h = x.reshape(rows, d)
    mem2 = mem.reshape(batch * mem_len, d)
    for l in range(depth):
        wl = w_in[l]
        w_conv = jnp.concatenate([_pad_head_cols(wl[:, :qk], GLA_HEADS, GLA_DK),
                                  _pad_head_cols(wl[:, qk:2 * qk], GLA_HEADS, GLA_DK),
                                  wl[:, 2 * qk:c1]], axis=1).astype(BF16)
        w_og = wl[:, c1:c2].astype(BF16)
        w_low = jnp.pad(wl[:, c2:c3], ((0, 0), (0, V7X_LANES - GLA_LOWRANK))).astype(BF16)
        w_sbq, w_sbk, w_sbv = (wl[:, a:b].astype(BF16) for a, b in ((c3, c4), (c4, c5), (c5, c5 + sbw)))
        cw = gla_conv_w[l]
        conv_w = jnp.concatenate([_pad_head_cols(cw[:, :qk], GLA_HEADS, GLA_DK),
                                  _pad_head_cols(cw[:, qk:2 * qk], GLA_HEADS, GLA_DK), cw[:, 2 * qk:]], axis=1)
        gate_w = jnp.pad(_pad_head_cols(gla_gate_w[l], GLA_HEADS, GLA_DK),
                         ((0, V7X_LANES - GLA_LOWRANK), (0, 0))).astype(BF16)
        gate_b = _pad_head_cols(gla_gate_b[l], GLA_HEADS, GLA_DK).reshape(1, -1)

        conv_in, og, low, sbq, sbk, sbv = _norm_matmul(
            h, mix_norm_g[l], [w_conv, w_og, w_low, w_sbq, w_sbk, w_sbv], tm=512)
        o_gla = _gla(conv_in, og, low, conv_w, gate_w, gate_b, gla_out_norm_g[l].reshape(1, -1), batch, seq)
        o_sb = _sb(sbq, sbk, sbv, batch, seq)

        wo = w_out[l].astype(BF16)
        h1, xq = _mix_out(h, o_gla, o_sb, sb_out_norm_g[l].reshape(1, -1), wo[:vw], wo[vw:],
                          xattn_norm_g[l].reshape(1, -1), w_xq[l].astype(BF16), tm=512)
        (kv,) = _norm_matmul(mem2, mem_norm_g[l], [w_xkv[l].astype(BF16)], tm=512)
        keys = peer_sub_keys[l].reshape(PEER_HEADS * 2, PEER_NKEYS, -1).astype(BF16)
        h2, xn, st = _xattn(xq, kv, h1, w_xo[l].astype(BF16), ffn_norm_g[l].reshape(1, -1),
                            peer_w_q[l].astype(BF16), keys, batch, seq, tm=512)
        st = st.reshape(PEER_HEADS, 2, PEER_NKEYS, rows)
        cnt, e1, rank2, e2 = _peer_sel(st, tt=256)
        h = _peer_ffn(xn, peer_u[l].astype(BF16), peer_v[l].T.astype(BF16), cnt, e1, rank2, e2, h2,
                      final_norm_g.reshape(1, -1), final_norm=(l == depth - 1), tt=512)
    return h.reshape(batch, seq, d)
```

```python
import functools

import jax
import jax.numpy as jnp
from jax import lax
from jax.experimental import pallas as pl
from jax.experimental.pallas import tpu as pltpu

EPS = 1e-6
GLA_HEADS = 4
GLA_DK = 64
GLA_DV = 128
GLA_LOWRANK = 16
GLA_GATE_NORM = 16.0
GLA_CHUNK = 64
CONV_WIDTH = 4
SB_HEADS = 8
SB_DH = 64
SB_BLOCK = 256
SB_LOG_WEIGHT_FLOOR = -104.0
X_HEADS = 4
X_DH = 256
PEER_HEADS = 8
PEER_NKEYS = 128
PEER_TOPK = 16

V7X_LANES = 128
V7X_SUBLANES = 8
V7X_VMEM_LIMIT_BYTES = 52 * 1024 * 1024

F32 = jnp.float32
BF16 = jnp.bfloat16
NEG_INF = float("-inf")


def _params(*semantics, flags=None):
    return pltpu.CompilerParams(dimension_semantics=semantics, vmem_limit_bytes=V7X_VMEM_LIMIT_BYTES, flags=flags)


def _dot(a, b):
    return jnp.dot(a, b, preferred_element_type=F32)


def _dot_nt(a, b):
    return lax.dot_general(a, b, (((1,), (1,)), ((), ())), preferred_element_type=F32)


def _dot_tn(a, b):
    return lax.dot_general(a, b, (((0,), (0,)), ((), ())), preferred_element_type=F32)


def _rms(x, g):
    return x * lax.rsqrt(jnp.mean(x * x, axis=-1, keepdims=True) + EPS) * g


def _sigmoid(x):
    return 1.0 / (1.0 + jnp.exp(-x))


def _log_sigmoid(x):
    return jnp.minimum(x, 0.0) - jnp.log(1.0 + jnp.exp(-jnp.abs(x)))


def _split_bf16(x):
    hi = x.astype(BF16)
    lo = (x - hi.astype(F32)).astype(BF16)
    return hi, lo


def _norm_matmul_kernel(x_ref, g_ref, *refs):
    n = len(refs) // 2
    xn = _rms(x_ref[...], g_ref[...]).astype(BF16)
    for w_ref, o_ref in zip(refs[:n], refs[n:]):
        o_ref[...] = _dot(xn, w_ref[...]).astype(o_ref.dtype)


def _norm_matmul(x, g, weights, tm):
    rows, d = x.shape
    tm = min(tm, rows)
    in_specs = [pl.BlockSpec((tm, d), lambda i: (i, 0)), pl.BlockSpec((1, d), lambda i: (0, 0))]
    in_specs += [pl.BlockSpec(w.shape, lambda i: (0, 0)) for w in weights]
    out_specs = [pl.BlockSpec((tm, w.shape[1]), lambda i: (i, 0)) for w in weights]
    out_shape = [jax.ShapeDtypeStruct((rows, w.shape[1]), BF16) for w in weights]
    return pl.pallas_call(
        _norm_matmul_kernel, grid=(rows // tm,), in_specs=in_specs, out_specs=out_specs,
        out_shape=out_shape, compiler_params=_params("parallel"), name="norm_matmul",
    )(x, g.reshape(1, d), *weights)


def _gla_kernel(x_ref, og_ref, low_ref, cw_ref, gw_ref, gb_ref, ng_ref, o_ref, halo_ref, st_ref, *, chunk):
    c = chunk
    seq = x_ref.shape[0]
    qk_w = GLA_HEADS * V7X_LANES
    halo_ref[...] = jnp.zeros_like(halo_ref)
    st_ref[...] = jnp.zeros_like(st_ref)
    ri = lax.broadcasted_iota(jnp.int32, (c, c), 0)
    ci = lax.broadcasted_iota(jnp.int32, (c, c), 1)
    causal = ri >= ci
    lmat = jnp.where(causal, 1.0, 0.0).astype(BF16)
    cw = cw_ref[...]
    ng = ng_ref[...]

    def body(n, carry):
        r0 = pl.multiple_of(n * c, c)
        cur = x_ref[pl.ds(r0, c), :].astype(F32)
        ext = jnp.concatenate([halo_ref[...], cur], axis=0)
        y = cur * cw[CONV_WIDTH - 1:CONV_WIDTH, :]
        for j in range(1, CONV_WIDTH):
            tap = cw[CONV_WIDTH - 1 - j:CONV_WIDTH - j, :]
            y = y + pltpu.roll(ext, j, 0)[V7X_SUBLANES:, :] * tap
        halo_ref[...] = cur[c - V7X_SUBLANES:, :]
        y = y * _sigmoid(y)

        gpre = _dot(low_ref[pl.ds(r0, c), :], gw_ref[...]) + gb_ref[...]
        g = _log_sigmoid(gpre) * (1.0 / GLA_GATE_NORM)
        g_hi, g_lo = _split_bf16(g)
        cum = _dot(lmat, g_hi) + _dot(lmat, g_lo)

        for h in range(GLA_HEADS):
            sl = slice(h * V7X_LANES, (h + 1) * V7X_LANES)
            q = y[:, sl] * (GLA_DK ** -0.5)
            k = y[:, qk_w + h * V7X_LANES:qk_w + (h + 1) * V7X_LANES]
            v = y[:, 2 * qk_w + h * GLA_DV:2 * qk_w + (h + 1) * GLA_DV].astype(BF16)
            cm = cum[:, sl]
            last = cm[c - 1:c, :]
            mid = cm[c // 2 - 1:c // 2, :]
            q_in = (q * jnp.exp(cm)).astype(BF16)
            q_mid = (q * jnp.exp(cm - mid)).astype(BF16)
            k_mid = (k * jnp.exp(mid - cm)).astype(BF16)
            k_end = (k * jnp.exp(last - cm)).astype(BF16)
            scores = jnp.where(causal, _dot_nt(q_mid, k_mid), 0.0)
            state = st_ref[h]
            o = _dot(scores.astype(BF16), v) + _dot_nt(q_in, state.astype(BF16))
            st_ref[h] = state * jnp.exp(last) + _dot_tn(v, k_end)
            og = og_ref[pl.ds(r0, c), h * GLA_DV:(h + 1) * GLA_DV].astype(F32)
            o_ref[pl.ds(r0, c), h * GLA_DV:(h + 1) * GLA_DV] = (_rms(o, ng) * (og * _sigmoid(og))).astype(o_ref.dtype)
        return carry

    lax.fori_loop(0, seq // c, body, 0)


def _gla(conv_in, og, low, conv_w, gate_w, gate_b, norm_g, batch, seq):
    w = conv_in.shape[1]
    vw = og.shape[1]
    qk_w = GLA_HEADS * V7X_LANES
    kern = functools.partial(_gla_kernel, chunk=GLA_CHUNK)
    return pl.pallas_call(
        kern, grid=(batch,),
        in_specs=[
            pl.BlockSpec((seq, w), lambda b: (b, 0)),
            pl.BlockSpec((seq, vw), lambda b: (b, 0)),
            pl.BlockSpec((seq, V7X_LANES), lambda b: (b, 0)),
            pl.BlockSpec((CONV_WIDTH, w), lambda b: (0, 0)),
            pl.BlockSpec((V7X_LANES, qk_w), lambda b: (0, 0)),
            pl.BlockSpec((1, qk_w), lambda b: (0, 0)),
            pl.BlockSpec((1, GLA_DV), lambda b: (0, 0)),
        ],
        out_specs=pl.BlockSpec((seq, vw), lambda b: (b, 0)),
        out_shape=jax.ShapeDtypeStruct((batch * seq, vw), BF16),
        scratch_shapes=[pltpu.VMEM((V7X_SUBLANES, w), F32), pltpu.VMEM((GLA_HEADS, GLA_DV, V7X_LANES), F32)],
        compiler_params=_params("parallel"), name="gla",
    )(conv_in, og, low, conv_w, gate_w, gate_b, norm_g)


def _sb_kernel(q_ref, k_ref, v_ref, o_ref, acc_ref, carry_ref, *, blk):
    qi = pl.program_id(2)
    scale = SB_DH ** -0.5
    lane = lax.broadcasted_iota(jnp.int32, (1, V7X_LANES), 1)
    q2 = q_ref[...]
    zero = jnp.zeros_like(q2)
    q_heads = (jnp.where(lane < SB_DH, q2, zero), jnp.where(lane >= SB_DH, q2, zero))
    ri = lax.broadcasted_iota(jnp.int32, (blk, blk), 0)
    ci = lax.broadcasted_iota(jnp.int32, (blk, blk), 1)
    strict = ci < ri
    ri2 = lax.broadcasted_iota(jnp.int32, (blk, 2 * blk), 0)
    ci2 = lax.broadcasted_iota(jnp.int32, (blk, 2 * blk), 1)
    suffix_ones = jnp.where((ri2 > ci2) | (ci2 >= blk), 1.0, 0.0).astype(BF16)
    acc_ref[...] = jnp.zeros_like(acc_ref)
    carry_ref[...] = jnp.zeros_like(carry_ref)

    def block(kb, masked):
        k0 = pl.multiple_of(kb * blk, blk)
        kblk = k_ref[pl.ds(k0, blk), :]
        vblk = v_ref[pl.ds(k0, blk), :]
        for hh in range(2):
            z = _dot_nt(q_heads[hh], kblk) * scale
            soft = jnp.log(1.0 + jnp.exp(-jnp.abs(z)))
            log_beta = jnp.minimum(z, 0.0) - soft
            log_rest = log_beta - z
            if masked:
                log_rest = jnp.where(strict, log_rest, 0.0)
            hi, lo = _split_bf16(log_rest)
            sums = _dot(hi, suffix_ones) + _dot(lo, suffix_ones)
            carry = carry_ref[hh]
            w = jnp.exp(log_beta + sums[:, :blk] + carry)
            if masked:
                w = jnp.where(strict, w, 0.0)
            acc_ref[hh] += _dot(w.astype(BF16), vblk)
            carry_ref[hh] = carry + sums[:, blk:]

    block(qi, True)

    def live():
        return jnp.max(carry_ref[:, :, :V7X_LANES]) > SB_LOG_WEIGHT_FLOOR

    def body(state):
        step, _ = state
        block(qi - 1 - step, False)
        return step + 1, live()

    lax.while_loop(lambda state: (state[0] < qi) & state[1], body, (jnp.int32(0), live()))
    o_ref[...] = jnp.where(lane < SB_DH, acc_ref[0], acc_ref[1]).astype(o_ref.dtype)


def _sb(q, k, v, batch, seq):
    blk = min(SB_BLOCK, seq)
    nq = seq // blk
    pairs = SB_HEADS // 2
    kern = functools.partial(_sb_kernel, blk=blk)
    return pl.pallas_call(
        kern, grid=(batch, pairs, nq),
        in_specs=[
            pl.BlockSpec((blk, V7X_LANES), lambda b, g, i: (b * nq + i, g)),
            pl.BlockSpec((seq, V7X_LANES), lambda b, g, i: (b, g)),
            pl.BlockSpec((seq, V7X_LANES), lambda b, g, i: (b, g)),
        ],
        out_specs=pl.BlockSpec((blk, V7X_LANES), lambda b, g, i: (b * nq + i, g)),
        out_shape=jax.ShapeDtypeStruct(q.shape, BF16),
        scratch_shapes=[pltpu.VMEM((2, blk, V7X_LANES), F32), pltpu.VMEM((2, blk, blk), F32)],
        compiler_params=_params("parallel", "parallel", "parallel"), name="stick_breaking",
    )(q, k, v)


def _mix_out_kernel(x_ref, og_ref, osb_ref, sbg_ref, wo_g_ref, wo_s_ref, xg_ref, wq_ref, h1_ref, q_ref):
    osb = _rms(osb_ref[...].astype(F32), sbg_ref[...]).astype(BF16)
    h1 = x_ref[...] + _dot(og_ref[...], wo_g_ref[...]) + _dot(osb, wo_s_ref[...])
    h1_ref[...] = h1
    q_ref[...] = _dot(_rms(h1, xg_ref[...]).astype(BF16), wq_ref[...]).astype(q_ref.dtype)


def _mix_out(x, o_gla, o_sb, sb_g, wo_g, wo_s, x_g, w_xq, tm):
    rows, d = x.shape
    tm = min(tm, rows)
    row = lambda w: pl.BlockSpec((tm, w), lambda i: (i, 0))
    full = lambda a: pl.BlockSpec(a.shape, lambda i: (0, 0))
    return pl.pallas_call(
        _mix_out_kernel, grid=(rows // tm,),
        in_specs=[row(d), row(o_gla.shape[1]), row(o_sb.shape[1]), full(sb_g), full(wo_g), full(wo_s), full(x_g), full(w_xq)],
        out_specs=[row(d), row(w_xq.shape[1])],
        out_shape=[jax.ShapeDtypeStruct((rows, d), F32), jax.ShapeDtypeStruct((rows, w_xq.shape[1]), BF16)],
        compiler_params=_params("parallel"), name="mix_out",
    )(x, o_gla, o_sb, sb_g, wo_g, wo_s, x_g, w_xq)


def _xattn_kernel(q_ref, kv_ref, h1_ref, wo_ref, fg_ref, wpq_ref, keys_ref, h2_ref, xn_ref, st_ref):
    xw = X_HEADS * X_DH
    outs = []
    for h in range(X_HEADS):
        qh = q_ref[:, h * X_DH:(h + 1) * X_DH]
        kh = kv_ref[:, h * X_DH:(h + 1) * X_DH]
        vh = kv_ref[:, xw + h * X_DH:xw + (h + 1) * X_DH]
        s = _dot_nt(qh, kh) * (X_DH ** -0.5)
        p = jnp.exp(s - jnp.max(s, axis=-1, keepdims=True))
        o = _dot(p.astype(BF16), vh) / jnp.sum(p, axis=-1, keepdims=True)
        outs.append(o.astype(BF16))
    h2 = h1_ref[...] + _dot(jnp.concatenate(outs, axis=-1), wo_ref[...])
    h2_ref[...] = h2
    xn = _rms(h2, fg_ref[...]).astype(BF16)
    xn_ref[...] = xn
    pq = _dot(xn, wpq_ref[...])
    half = keys_ref.shape[2]
    for hp in range(keys_ref.shape[0]):
        st_ref[hp] = _dot_nt(keys_ref[hp], pq[:, hp * half:(hp + 1) * half].astype(BF16))


def _xattn(q, kv, h1, w_xo, f_g, w_pq, keys, batch, seq, tm):
    rows, d = h1.shape
    tm = min(tm, seq)
    nt = seq // tm
    mem_len = kv.shape[0] // batch
    row = lambda w: pl.BlockSpec((tm, w), lambda b, i: (b * nt + i, 0))
    full = lambda a: pl.BlockSpec(a.shape, lambda b, i: (0,) * a.ndim)
    n_hp, n_keys, _ = keys.shape
    return pl.pallas_call(
        _xattn_kernel, grid=(batch, nt),
        in_specs=[row(q.shape[1]), pl.BlockSpec((mem_len, kv.shape[1]), lambda b, i: (b, 0)), row(d),
                  full(w_xo), full(f_g), full(w_pq), full(keys)],
        out_specs=[row(d), row(d), pl.BlockSpec((n_hp, n_keys, tm), lambda b, i: (0, 0, b * nt + i))],
        out_shape=[jax.ShapeDtypeStruct((rows, d), F32), jax.ShapeDtypeStruct((rows, d), BF16),
                   jax.ShapeDtypeStruct((n_hp, n_keys, rows), F32)],
        compiler_params=_params("parallel", "parallel"), name="xattn_peer_scores",
    )(q, kv, h1, w_xo, f_g, w_pq, keys)


def _peer_sel_kernel(st_ref, cnt_ref, e1_ref, rank2_ref, e2_ref):
    n_top = PEER_TOPK + 1
    not_ranked = float(PEER_NKEYS - 1)

    def top_values(x):
        vals = []
        rank = jnp.full(x.shape, not_ranked, F32)
        for r in range(n_top):
            m = jnp.max(x, axis=0, keepdims=True)
            hit = x == m
            vals.append(m)
            rank = jnp.where(hit, float(r), rank)
            x = jnp.where(hit, NEG_INF, x)
        return vals, rank

    tops = [[top_values(st_ref[h, p]) for h in range(PEER_HEADS)] for p in range(2)]
    packed = [[jnp.concatenate([tops[p][h][0][r] for h in range(PEER_HEADS)], axis=0) for r in range(n_top)]
              for p in range(2)]
    pairs = [(i, j) for i in range(n_top) for j in range(n_top) if (i + 1) * (j + 1) <= n_top]
    sums = {ij: packed[0][ij[0]] + packed[1][ij[1]] for ij in pairs}
    cands = [sums[ij] for ij in pairs]
    best = []
    for _ in range(n_top):
        m = functools.reduce(jnp.maximum, cands)
        best.append(m)
        cands = [jnp.where(c == m, NEG_INF, c) for c in cands]
    thr = 0.5 * (best[PEER_TOPK - 1] + best[PEER_TOPK])
    z = functools.reduce(lambda a, b: a + b, [jnp.exp(b - best[0]) for b in best[:PEER_TOPK]])
    inv_z = 1.0 / z
    cnt_by_rank = [functools.reduce(lambda a, b: a + b,
                                    [jnp.where(sums[(i, j)] >= thr, 1.0, 0.0) for j in range(n_top) if (i, j) in sums])
                   for i in range(PEER_TOPK)]
    for h in range(PEER_HEADS):
        rank1 = tops[0][h][1]
        cnt = jnp.zeros(rank1.shape, F32)
        for i in range(PEER_TOPK):
            cnt = jnp.where(rank1 == float(i), cnt_by_rank[i][h:h + 1, :], cnt)
        cnt_ref[h] = cnt
        e1_ref[h] = jnp.exp(st_ref[h, 0] - tops[0][h][0][0]) * inv_z[h:h + 1, :]
        rank2_ref[h] = pltpu.bitcast(tops[1][h][1].astype(BF16), jnp.uint32)
        e2_ref[h] = pltpu.bitcast(jnp.exp(st_ref[h, 1] - tops[1][h][0][0]).astype(BF16), jnp.uint32)


def _peer_sel(st, tt):
    heads, _, n_keys, rows = st.shape
    tt = min(tt, rows)
    spec = pl.BlockSpec((heads, n_keys, tt), lambda i: (0, 0, i))
    word_spec = pl.BlockSpec((heads, n_keys // 2, tt), lambda i: (0, 0, i))
    f32_shape = jax.ShapeDtypeStruct((heads, n_keys, rows), F32)
    word_shape = jax.ShapeDtypeStruct((heads, n_keys // 2, rows), jnp.uint32)
    return pl.pallas_call(
        _peer_sel_kernel, grid=(rows // tt,),
        in_specs=[pl.BlockSpec((heads, 2, n_keys, tt), lambda i: (0, 0, 0, i))],
        out_specs=[spec, spec, word_spec, word_spec], out_shape=[f32_shape, f32_shape, word_shape, word_shape],
        compiler_params=_params("parallel"), name="peer_select",
    )(st)


def _gelu_tanh(x):
    return 0.5 * x * (1.0 + jnp.tanh(0.7978845608028654 * (x + 0.044715 * (x * x * x))))


def _peer_ffn_kernel(xn_ref, u_ref, vt_ref, cnt_ref, e1_ref, rank2_ref, e2_ref, h2_ref, fg_ref, o_ref,
                     acc_ref, act0_ref, act1_ref, gate0_ref, gate1_ref, *, n_groups, final_norm):
    s = pl.program_id(0)
    n_items = pl.num_programs(0) - 2
    ec, tt = act0_ref.shape
    nk = PEER_NKEYS
    pack = 2 * V7X_SUBLANES
    assert ec == V7X_SUBLANES * nk

    @pl.when(s == 0)
    def _():
        for ref in (act0_ref, act1_ref, gate0_ref, gate1_ref):
            ref[...] = jnp.zeros_like(ref)

    grp_c = jnp.clip(s - 2, 0, n_items - 1) % n_groups

    @pl.when(grp_c == 0)
    def _():
        acc_ref[...] = jnp.zeros_like(acc_ref)

    grp = jnp.clip(s - 1, 0, n_items - 1) % n_groups

    def gate_tile(act_old, gate_new, c, tls):
        lanes = [slice(tl * V7X_LANES, (tl + 1) * V7X_LANES) for tl in tls]
        w = [[None] * len(lanes) for _ in range(nk // pack)]
        for h in range(PEER_HEADS):
            cnt_b = [jnp.broadcast_to(cnt_ref[h, grp, c:c + 1, ts], (pack, V7X_LANES)).astype(BF16) for ts in lanes]
            e1_b = [jnp.broadcast_to(e1_ref[h, grp, c:c + 1, ts], (pack, V7X_LANES)).astype(BF16) for ts in lanes]
            for rg in range(nk // pack):
                ws = slice(rg * V7X_SUBLANES, (rg + 1) * V7X_SUBLANES)
                for k, ts in enumerate(lanes):
                    e2 = pltpu.bitcast(e2_ref[h, ws, ts], BF16)
                    rank2 = pltpu.bitcast(rank2_ref[h, ws, ts], BF16)
                    term = jnp.where(rank2 < cnt_b[k], e2, jnp.zeros_like(e2)) * e1_b[k]
                    w[rg][k] = term if w[rg][k] is None else w[rg][k] + term
        for rg in range(nk // pack):
            rows = slice(c * nk + rg * pack, c * nk + (rg + 1) * pack)
            for k, ts in enumerate(lanes):
                gate_new[rows, ts] = _gelu_tanh(act_old[rows, ts]).astype(BF16) * w[rg][k]

    def stages(act_new, act_old, gate_new, gate_old):
        n_pieces = 2
        half = tt // n_pieces
        lane_tiles = tt // V7X_LANES
        tiles = [(c, (tl, tl + 1)) for c in range(V7X_SUBLANES) for tl in range(0, lane_tiles, 2)]
        per_piece = len(tiles) // (2 * n_pieces)
        for p in range(n_pieces):
            cols = slice(p * half, (p + 1) * half)
            act_new[:, cols] = _dot_nt(u_ref[...], xn_ref[cols, :])
            for c, tls in tiles[2 * p * per_piece:(2 * p + 1) * per_piece]:
                gate_tile(act_old, gate_new, c, tls)
            acc_ref[:, cols] += _dot(vt_ref[...], gate_old[:, cols])
            for c, tls in tiles[(2 * p + 1) * per_piece:(2 * p + 2) * per_piece]:
                gate_tile(act_old, gate_new, c, tls)

    @pl.when(s % 2 == 0)
    def _():
        stages(act0_ref, act1_ref, gate1_ref, gate0_ref)

    @pl.when(s % 2 == 1)
    def _():
        stages(act1_ref, act0_ref, gate0_ref, gate1_ref)

    @pl.when((grp_c == n_groups - 1) & (s >= 2))
    def _():
        out = h2_ref[...] + acc_ref[...].T
        o_ref[...] = _rms(out, fg_ref[...]) if final_norm else out


def _peer_ffn(xn, u, vt, cnt, e1, rank2, e2, h2, f_g, final_norm, tt):
    rows, dm = xn.shape
    tt = min(tt, rows)
    heads, n_keys, _ = cnt.shape
    ec = V7X_SUBLANES * n_keys
    n_groups = u.shape[0] // ec
    n_items = (rows // tt) * n_groups
    item = lambda s, lag: jnp.clip(s - lag, 0, n_items - 1)
    word_spec = pl.BlockSpec((heads, n_keys // 2, tt), lambda s: (0, 0, item(s, 1) // n_groups))
    grp_spec = pl.BlockSpec((heads, n_keys // V7X_SUBLANES, V7X_SUBLANES, tt),
                            lambda s: (0, 0, 0, item(s, 1) // n_groups))
    cnt = cnt.reshape(heads, n_keys // V7X_SUBLANES, V7X_SUBLANES, rows)
    e1 = e1.reshape(heads, n_keys // V7X_SUBLANES, V7X_SUBLANES, rows)
    kern = functools.partial(_peer_ffn_kernel, n_groups=n_groups, final_norm=final_norm)
    return pl.pallas_call(
        kern, grid=(n_items + 2,),
        in_specs=[
            pl.BlockSpec((tt, dm), lambda s: (item(s, 0) // n_groups, 0)),
            pl.BlockSpec((ec, dm), lambda s: (item(s, 0) % n_groups, 0)),
            pl.BlockSpec((dm, ec), lambda s: (0, item(s, 2) % n_groups)),
            grp_spec, grp_spec, word_spec, word_spec,
            pl.BlockSpec((tt, dm), lambda s: (item(s, 2) // n_groups, 0)),
            pl.BlockSpec((1, dm), lambda s: (0, 0)),
        ],
        out_specs=pl.BlockSpec((tt, dm), lambda s: (item(s, 2) // n_groups, 0)),
        out_shape=jax.ShapeDtypeStruct((rows, dm), F32),
        scratch_shapes=[pltpu.VMEM((dm, tt), F32), pltpu.VMEM((ec, tt), F32), pltpu.VMEM((ec, tt), F32),
                        pltpu.VMEM((ec, tt), BF16), pltpu.VMEM((ec, tt), BF16)],
        compiler_params=_params("arbitrary", flags=None), name="peer_ffn",
    )(xn, u, vt, cnt, e1, rank2, e2, h2, f_g)


def _pad_head_cols(w, heads, width):
    lead = w.shape[:-1]
    w = w.reshape(lead + (heads, width))
    w = jnp.pad(w, [(0, 0)] * len(lead) + [(0, 0), (0, V7X_LANES - width)])
    return w.reshape(lead + (heads * V7X_LANES,))


def kernel(x, mem, mix_norm_g, w_in, gla_conv_w, gla_gate_w, gla_gate_b, gla_out_norm_g, sb_out_norm_g, w_out,
           xattn_norm_g, mem_norm_g, w_xq, w_xkv, w_xo, ffn_norm_g, peer_w_q, peer_sub_keys, peer_u, peer_v,
           final_norm_g):
    batch, seq, d = x.shape
    mem_len = mem.shape[1]
    rows = batch * seq
    depth = w_in.shape[0]
    qk = GLA_HEADS * GLA_DK
    vw = GLA_HEADS * GLA_DV
    sbw = SB_HEADS * SB_DH
    c1 = 2 * qk + vw
    c2 = c1 + vw
    c3 = c2 + GLA_LOWRANK
    c4 = c3 + sbw
    c5 = c4 + sbw

    h = x.reshape(rows, d)
    mem2 = mem.reshape(batch * mem_len, d)
    for l in range(depth):
        wl = w_in[l]
        w_conv = jnp.concatenate([_pad_head_cols(wl[:, :qk], GLA_HEADS, GLA_DK),
                                  _pad_head_cols(wl[:, qk:2 * qk], GLA_HEADS, GLA_DK),
                                  wl[:, 2 * qk:c1]], axis=1).astype(BF16)
        w_og = wl[:, c1:c2].astype(BF16)
        w_low = jnp.pad(wl[:, c2:c3], ((0, 0), (0, V7X_LANES - GLA_LOWRANK))).astype(BF16)
        w_sbq, w_sbk, w_sbv = (wl[:, a:b].astype(BF16) for a, b in ((c3, c4), (c4, c5), (c5, c5 + sbw)))
        cw = gla_conv_w[l]
        conv_w = jnp.concatenate([_pad_head_cols(cw[:, :qk], GLA_HEADS, GLA_DK),
                                  _pad_head_cols(cw[:, qk:2 * qk], GLA_HEADS, GLA_DK), cw[:, 2 * qk:]], axis=1)
        gate_w = jnp.pad(_pad_head_cols(gla_gate_w[l], GLA_HEADS, GLA_DK),
                         ((0, V7X_LANES - GLA_LOWRANK), (0, 0))).astype(BF16)
        gate_b = _pad_head_cols(gla_gate_b[l], GLA_HEADS, GLA_DK).reshape(1, -1)

        conv_in, og, low, sbq, sbk, sbv = _norm_matmul(
            h, mix_norm_g[l], [w_conv, w_og, w_low, w_sbq, w_sbk, w_sbv], tm=512)
        o_gla = _gla(conv_in, og, low, conv_w, gate_w, gate_b, gla_out_norm_g[l].reshape(1, -1), batch, seq)
        o_sb = _sb(sbq, sbk, sbv, batch, seq)

        wo = w_out[l].astype(BF16)
        h1, xq = _mix_out(h, o_gla, o_sb, sb_out_norm_g[l].reshape(1, -1), wo[:vw], wo[vw:],
                          xattn_norm_g[l].reshape(1, -1), w_xq[l].astype(BF16), tm=512)
        (kv,) = _norm_matmul(mem2, mem_norm_g[l], [w_xkv[l].astype(BF16)], tm=512)
        keys = peer_sub_keys[l].reshape(PEER_HEADS * 2, PEER_NKEYS, -1).astype(BF16)
        h2, xn, st = _xattn(xq, kv, h1, w_xo[l].astype(BF16), ffn_norm_g[l].reshape(1, -1),
                            peer_w_q[l].astype(BF16), keys, batch, seq, tm=512)
        st = st.reshape(PEER_HEADS, 2, PEER_NKEYS, rows)
        cnt, e1, rank2, e2 = _peer_sel(st, tt=256)
        h = _peer_ffn(xn, peer_u[l].astype(BF16), peer_v[l].T.astype(BF16), cnt, e1, rank2, e2, h2,
                      final_norm_g.reshape(1, -1), final_norm=(l == depth - 1), tt=512)
    return h.reshape(batch, seq, d)
```

```python
import functools

import jax
import jax.numpy as jnp
from jax import lax
from jax.experimental import pallas as pl
from jax.experimental.pallas import tpu as pltpu

EPS = 1e-6
GLA_HEADS = 4
GLA_DK = 64
GLA_DV = 128
GLA_LOWRANK = 16
GLA_GATE_NORM = 16.0
GLA_CHUNK = 128
CONV_WIDTH = 4
SB_HEADS = 8
SB_DH = 64
SB_BLOCK = 256
SB_LOG_WEIGHT_FLOOR = -104.0
X_HEADS = 4
X_DH = 256
PEER_HEADS = 8
PEER_NKEYS = 128
PEER_TOPK = 16

V7X_LANES = 128
V7X_SUBLANES = 8
V7X_VMEM_LIMIT_BYTES = 52 * 1024 * 1024

F32 = jnp.float32
BF16 = jnp.bfloat16
NEG_INF = float("-inf")


def _params(*semantics, flags=None):
    return pltpu.CompilerParams(dimension_semantics=semantics, vmem_limit_bytes=V7X_VMEM_LIMIT_BYTES, flags=flags)


def _dot(a, b):
    return jnp.dot(a, b, preferred_element_type=F32)


def _dot_nt(a, b):
    return lax.dot_general(a, b, (((1,), (1,)), ((), ())), preferred_element_type=F32)


def _dot_tn(a, b):
    return lax.dot_general(a, b, (((0,), (0,)), ((), ())), preferred_element_type=F32)


def _rms(x, g):
    return x * lax.rsqrt(jnp.mean(x * x, axis=-1, keepdims=True) + EPS) * g


def _sigmoid(x):
    return 1.0 / (1.0 + jnp.exp(-x))


def _log_sigmoid(x):
    return jnp.minimum(x, 0.0) - jnp.log(1.0 + jnp.exp(-jnp.abs(x)))


def _split_bf16(x):
    hi = x.astype(BF16)
    lo = (x - hi.astype(F32)).astype(BF16)
    return hi, lo


def _norm_matmul_kernel(x_ref, g_ref, *refs):
    n = len(refs) // 2
    xn = _rms(x_ref[...], g_ref[...]).astype(BF16)
    for w_ref, o_ref in zip(refs[:n], refs[n:]):
        o_ref[...] = _dot(xn, w_ref[...]).astype(o_ref.dtype)


def _norm_matmul(x, g, weights, tm):
    rows, d = x.shape
    tm = min(tm, rows)
    in_specs = [pl.BlockSpec((tm, d), lambda i: (i, 0)), pl.BlockSpec((1, d), lambda i: (0, 0))]
    in_specs += [pl.BlockSpec(w.shape, lambda i: (0, 0)) for w in weights]
    out_specs = [pl.BlockSpec((tm, w.shape[1]), lambda i: (i, 0)) for w in weights]
    out_shape = [jax.ShapeDtypeStruct((rows, w.shape[1]), BF16) for w in weights]
    return pl.pallas_call(
        _norm_matmul_kernel, grid=(rows // tm,), in_specs=in_specs, out_specs=out_specs,
        out_shape=out_shape, compiler_params=_params("parallel"), name="norm_matmul",
    )(x, g.reshape(1, d), *weights)


def _gla_kernel(x_ref, og_ref, low_ref, cw_ref, gw_ref, gb_ref, ng_ref, o_ref, halo_ref, st_ref, *, chunk):
    c = chunk
    seq = x_ref.shape[0]
    qk_w = GLA_HEADS * V7X_LANES
    halo_ref[...] = jnp.zeros_like(halo_ref)
    st_ref[...] = jnp.zeros_like(st_ref)
    ri = lax.broadcasted_iota(jnp.int32, (c, c), 0)
    ci = lax.broadcasted_iota(jnp.int32, (c, c), 1)
    causal = ri >= ci
    lmat = jnp.where(causal, 1.0, 0.0).astype(BF16)
    cw = cw_ref[...]
    ng = ng_ref[...]

    def body(n, carry):
        r0 = pl.multiple_of(n * c, c)
        cur = x_ref[pl.ds(r0, c), :].astype(F32)
        ext = jnp.concatenate([halo_ref[...], cur], axis=0)
        y = cur * cw[CONV_WIDTH - 1:CONV_WIDTH, :]
        for j in range(1, CONV_WIDTH):
            tap = cw[CONV_WIDTH - 1 - j:CONV_WIDTH - j, :]
            y = y + pltpu.roll(ext, j, 0)[V7X_SUBLANES:, :] * tap
        halo_ref[...] = cur[c - V7X_SUBLANES:, :]
        y = y * _sigmoid(y)

        gpre = _dot(low_ref[pl.ds(r0, c), :], gw_ref[...]) + gb_ref[...]
        g = _log_sigmoid(gpre) * (1.0 / GLA_GATE_NORM)
        g_hi, g_lo = _split_bf16(g)
        cum = _dot(lmat, g_hi) + _dot(lmat, g_lo)

        for h in range(GLA_HEADS):
            sl = slice(h * V7X_LANES, (h + 1) * V7X_LANES)
            q = y[:, sl] * (GLA_DK ** -0.5)
            k = y[:, qk_w + h * V7X_LANES:qk_w + (h + 1) * V7X_LANES]
            v = y[:, 2 * qk_w + h * GLA_DV:2 * qk_w + (h + 1) * GLA_DV].astype(BF16)
            cm = cum[:, sl]
            last = cm[c - 1:c, :]
            mid = cm[c // 2 - 1:c // 2, :]
            q_in = (q * jnp.exp(cm)).astype(BF16)
            q_mid = (q * jnp.exp(cm - mid)).astype(BF16)
            k_mid = (k * jnp.exp(mid - cm)).astype(BF16)
            k_end = (k * jnp.exp(last - cm)).astype(BF16)
            scores = jnp.where(causal, _dot_nt(q_mid, k_mid), 0.0)
            state = st_ref[h]
            o = _dot(scores.astype(BF16), v) + _dot_nt(q_in, state.astype(BF16))
            st_ref[h] = state * jnp.exp(last) + _dot_tn(v, k_end)
            og = og_ref[pl.ds(r0, c), h * GLA_DV:(h + 1) * GLA_DV].astype(F32)
            o_ref[pl.ds(r0, c), h * GLA_DV:(h + 1) * GLA_DV] = (_rms(o, ng) * (og * _sigmoid(og))).astype(o_ref.dtype)
        return carry

    lax.fori_loop(0, seq // c, body, 0)


def _gla(conv_in, og, low, conv_w, gate_w, gate_b, norm_g, batch, seq):
    w = conv_in.shape[1]
    vw = og.shape[1]
    qk_w = GLA_HEADS * V7X_LANES
    kern = functools.partial(_gla_kernel, chunk=GLA_CHUNK)
    return pl.pallas_call(
        kern, grid=(batch,),
        in_specs=[
            pl.BlockSpec((seq, w), lambda b: (b, 0)),
            pl.BlockSpec((seq, vw), lambda b: (b, 0)),
            pl.BlockSpec((seq, V7X_LANES), lambda b: (b, 0)),
            pl.BlockSpec((CONV_WIDTH, w), lambda b: (0, 0)),
            pl.BlockSpec((V7X_LANES, qk_w), lambda b: (0, 0)),
            pl.BlockSpec((1, qk_w), lambda b: (0, 0)),
            pl.BlockSpec((1, GLA_DV), lambda b: (0, 0)),
        ],
        out_specs=pl.BlockSpec((seq, vw), lambda b: (b, 0)),
        out_shape=jax.ShapeDtypeStruct((batch * seq, vw), BF16),
        scratch_shapes=[pltpu.VMEM((V7X_SUBLANES, w), F32), pltpu.VMEM((GLA_HEADS, GLA_DV, V7X_LANES), F32)],
        compiler_params=_params("parallel"), name="gla",
    )(conv_in, og, low, conv_w, gate_w, gate_b, norm_g)


def _sb_kernel(q_ref, k_ref, v_ref, o_ref, acc_ref, carry_ref, *, blk):
    qi = pl.program_id(2)
    scale = SB_DH ** -0.5
    lane = lax.broadcasted_iota(jnp.int32, (1, V7X_LANES), 1)
    q2 = q_ref[...]
    zero = jnp.zeros_like(q2)
    q_heads = (jnp.where(lane < SB_DH, q2, zero), jnp.where(lane >= SB_DH, q2, zero))
    ri = lax.broadcasted_iota(jnp.int32, (blk, blk), 0)
    ci = lax.broadcasted_iota(jnp.int32, (blk, blk), 1)
    strict = ci < ri
    ri2 = lax.broadcasted_iota(jnp.int32, (blk, 2 * blk), 0)
    ci2 = lax.broadcasted_iota(jnp.int32, (blk, 2 * blk), 1)
    suffix_ones = jnp.where((ri2 > ci2) | (ci2 >= blk), 1.0, 0.0).astype(BF16)
    def block(kb, carry, acc, diagonal=False, keep=None):
        k0 = pl.multiple_of(kb * blk, blk)
        kblk = k_ref[pl.ds(k0, blk), :]
        vblk = v_ref[pl.ds(k0, blk), :]
        new_carry, new_acc = [], []
        for hh in range(2):
            z = _dot_nt(q_heads[hh], kblk) * scale
            soft = jnp.log(1.0 + jnp.exp(-jnp.abs(z)))
            log_beta = jnp.minimum(z, 0.0) - soft
            log_rest = log_beta - z
            if diagonal:
                log_rest = jnp.where(strict, log_rest, 0.0)
            hi, lo = _split_bf16(log_rest)
            sums = _dot(hi, suffix_ones) + _dot(lo, suffix_ones)
            log_w = log_beta + sums[:, :blk]
            w = jnp.exp(log_w if carry[hh] is None else log_w + carry[hh])
            if diagonal:
                w = jnp.where(strict, w, 0.0)
            total = sums[:, blk:]
            if keep is not None:
                w = w * keep
                total = total * keep
            out = _dot(w.astype(BF16), vblk)
            new_acc.append(out if acc[hh] is None else acc[hh] + out)
            new_carry.append(total if carry[hh] is None else carry[hh] + total)
        return new_carry, new_acc

    carry, acc = block(qi, [None, None], [None, None], diagonal=True)
    carry, acc = block(jnp.maximum(qi - 1, 0), carry, acc, keep=(qi > 0).astype(F32))
    for hh in range(2):
        carry_ref[hh] = carry[hh]
        acc_ref[hh] = acc[hh]

    def live():
        return jnp.max(carry_ref[:, :, :V7X_LANES]) > SB_LOG_WEIGHT_FLOOR

    def body(state):
        step, _ = state
        carry, acc = block(qi - 1 - step, [carry_ref[0], carry_ref[1]], [acc_ref[0], acc_ref[1]])
        for hh in range(2):
            carry_ref[hh] = carry[hh]
            acc_ref[hh] = acc[hh]
        return step + 1, live()

    lax.while_loop(lambda state: (state[0] < qi) & state[1], body, (jnp.int32(1), live()))
    o_ref[...] = jnp.where(lane < SB_DH, acc_ref[0], acc_ref[1]).astype(o_ref.dtype)


def _sb(q, k, v, batch, seq):
    blk = min(SB_BLOCK, seq)
    nq = seq // blk
    pairs = SB_HEADS // 2
    kern = functools.partial(_sb_kernel, blk=blk)
    return pl.pallas_call(
        kern, grid=(batch, pairs, nq),
        in_specs=[
            pl.BlockSpec((blk, V7X_LANES), lambda b, g, i: (b * nq + i, g)),
            pl.BlockSpec((seq, V7X_LANES), lambda b, g, i: (b, g)),
            pl.BlockSpec((seq, V7X_LANES), lambda b, g, i: (b, g)),
        ],
        out_specs=pl.BlockSpec((blk, V7X_LANES), lambda b, g, i: (b * nq + i, g)),
        out_shape=jax.ShapeDtypeStruct(q.shape, BF16),
        scratch_shapes=[pltpu.VMEM((2, blk, V7X_LANES), F32), pltpu.VMEM((2, blk, blk), F32)],
        compiler_params=_params("parallel", "parallel", "parallel"), name="stick_breaking",
    )(q, k, v)


def _mix_out_kernel(x_ref, og_ref, osb_ref, sbg_ref, wo_g_ref, wo_s_ref, xg_ref, wq_ref, h1_ref, q_ref):
    osb = _rms(osb_ref[...].astype(F32), sbg_ref[...]).astype(BF16)
    h1 = x_ref[...] + _dot(og_ref[...], wo_g_ref[...]) + _dot(osb, wo_s_ref[...])
    h1_ref[...] = h1
    q_ref[...] = _dot(_rms(h1, xg_ref[...]).astype(BF16), wq_ref[...]).astype(q_ref.dtype)


def _mix_out(x, o_gla, o_sb, sb_g, wo_g, wo_s, x_g, w_xq, tm):
    rows, d = x.shape
    tm = min(tm, rows)
    row = lambda w: pl.BlockSpec((tm, w), lambda i: (i, 0))
    full = lambda a: pl.BlockSpec(a.shape, lambda i: (0, 0))
    return pl.pallas_call(
        _mix_out_kernel, grid=(rows // tm,),
        in_specs=[row(d), row(o_gla.shape[1]), row(o_sb.shape[1]), full(sb_g), full(wo_g), full(wo_s), full(x_g), full(w_xq)],
        out_specs=[row(d), row(w_xq.shape[1])],
        out_shape=[jax.ShapeDtypeStruct((rows, d), F32), jax.ShapeDtypeStruct((rows, w_xq.shape[1]), BF16)],
        compiler_params=_params("parallel"), name="mix_out",
    )(x, o_gla, o_sb, sb_g, wo_g, wo_s, x_g, w_xq)


def _xattn_kernel(q_ref, kv_ref, h1_ref, wo_ref, fg_ref, wpq_ref, keys_ref, h2_ref, xn_ref, st_ref):
    xw = X_HEADS * X_DH
    outs = []
    for h in range(X_HEADS):
        qh = q_ref[:, h * X_DH:(h + 1) * X_DH]
        kh = kv_ref[:, h * X_DH:(h + 1) * X_DH]
        vh = kv_ref[:, xw + h * X_DH:xw + (h + 1) * X_DH]
        s = _dot_nt(qh, kh) * (X_DH ** -0.5)
        p = jnp.exp(s - jnp.max(s, axis=-1, keepdims=True))
        o = _dot(p.astype(BF16), vh) / jnp.sum(p, axis=-1, keepdims=True)
        outs.append(o.astype(BF16))
    h2 = h1_ref[...] + _dot(jnp.concatenate(outs, axis=-1), wo_ref[...])
    h2_ref[...] = h2
    xn = _rms(h2, fg_ref[...]).astype(BF16)
    xn_ref[...] = xn
    pq = _dot(xn, wpq_ref[...])
    half = keys_ref.shape[2]
    for hp in range(keys_ref.shape[0]):
        st_ref[hp] = _dot_nt(keys_ref[hp], pq[:, hp * half:(hp + 1) * half].astype(BF16))


def _xattn(q, kv, h1, w_xo, f_g, w_pq, keys, batch, seq, tm):
    rows, d = h1.shape
    tm = min(tm, seq)
    nt = seq // tm
    mem_len = kv.shape[0] // batch
    row = lambda w: pl.BlockSpec((tm, w), lambda b, i: (b * nt + i, 0))
    full = lambda a: pl.BlockSpec(a.shape, lambda b, i: (0,) * a.ndim)
    n_hp, n_keys, _ = keys.shape
    return pl.pallas_call(
        _xattn_kernel, grid=(batch, nt),
        in_specs=[row(q.shape[1]), pl.BlockSpec((mem_len, kv.shape[1]), lambda b, i: (b, 0)), row(d),
                  full(w_xo), full(f_g), full(w_pq), full(keys)],
        out_specs=[row(d), row(d), pl.BlockSpec((n_hp, n_keys, tm), lambda b, i: (0, 0, b * nt + i))],
        out_shape=[jax.ShapeDtypeStruct((rows, d), F32), jax.ShapeDtypeStruct((rows, d), BF16),
                   jax.ShapeDtypeStruct((n_hp, n_keys, rows), F32)],
        compiler_params=_params("parallel", "parallel"), name="xattn_peer_scores",
    )(q, kv, h1, w_xo, f_g, w_pq, keys)


def _peer_sel_kernel(st_ref, cnt_ref, e1_ref, rank2_ref, e2_ref):
    n_top = PEER_TOPK + 1
    not_ranked = float(PEER_NKEYS - 1)

    def top_values(x):
        vals = []
        rank = jnp.full(x.shape, not_ranked, F32)
        for r in range(n_top):
            m = jnp.max(x, axis=0, keepdims=True)
            hit = x == m
            vals.append(m)
            rank = jnp.where(hit, float(r), rank)
            x = jnp.where(hit, NEG_INF, x)
        return vals, rank

    tops = [[top_values(st_ref[h, p]) for h in range(PEER_HEADS)] for p in range(2)]
    packed = [[jnp.concatenate([tops[p][h][0][r] for h in range(PEER_HEADS)], axis=0) for r in range(n_top)]
              for p in range(2)]
    pairs = [(i, j) for i in range(n_top) for j in range(n_top) if (i + 1) * (j + 1) <= n_top]
    sums = {ij: packed[0][ij[0]] + packed[1][ij[1]] for ij in pairs}
    cands = [sums[ij] for ij in pairs]
    best = []
    for _ in range(n_top):
        m = functools.reduce(jnp.maximum, cands)
        best.append(m)
        cands = [jnp.where(c == m, NEG_INF, c) for c in cands]
    thr = 0.5 * (best[PEER_TOPK - 1] + best[PEER_TOPK])
    z = functools.reduce(lambda a, b: a + b, [jnp.exp(b - best[0]) for b in best[:PEER_TOPK]])
    inv_z = 1.0 / z
    cnt_by_rank = [functools.reduce(lambda a, b: a + b,
                                    [jnp.where(sums[(i, j)] >= thr, 1.0, 0.0) for j in range(n_top) if (i, j) in sums])
                   for i in range(PEER_TOPK)]
    for h in range(PEER_HEADS):
        rank1 = tops[0][h][1]
        cnt = jnp.zeros(rank1.shape, F32)
        for i in range(PEER_TOPK):
            cnt = jnp.where(rank1 == float(i), cnt_by_rank[i][h:h + 1, :], cnt)
        cnt_ref[h] = cnt
        e1_ref[h] = jnp.exp(st_ref[h, 0] - tops[0][h][0][0]) * inv_z[h:h + 1, :]
        rank2_ref[h] = pltpu.bitcast(tops[1][h][1].astype(BF16), jnp.uint32)
        e2_ref[h] = pltpu.bitcast(jnp.exp(st_ref[h, 1] - tops[1][h][0][0]).astype(BF16), jnp.uint32)


def _peer_sel(st, tt):
    heads, _, n_keys, rows = st.shape
    tt = min(tt, rows)
    spec = pl.BlockSpec((heads, n_keys, tt), lambda i: (0, 0, i))
    word_spec = pl.BlockSpec((heads, n_keys // 2, tt), lambda i: (0, 0, i))
    f32_shape = jax.ShapeDtypeStruct((heads, n_keys, rows), F32)
    word_shape = jax.ShapeDtypeStruct((heads, n_keys // 2, rows), jnp.uint32)
    return pl.pallas_call(
        _peer_sel_kernel, grid=(rows // tt,),
        in_specs=[pl.BlockSpec((heads, 2, n_keys, tt), lambda i: (0, 0, 0, i))],
        out_specs=[spec, spec, word_spec, word_spec], out_shape=[f32_shape, f32_shape, word_shape, word_shape],
        compiler_params=_params("parallel"), name="peer_select",
    )(st)


def _gelu_tanh(x):
    k = 2.0 * 0.7978845608028654 * 1.4426950408889634
    return x / (1.0 + jnp.exp2(x * (-k - (k * 0.044715) * (x * x))))


def _peer_ffn_kernel(xn_ref, u_ref, vt_ref, cnt_ref, e1_ref, rank2_ref, e2_ref, h2_ref, fg_ref, o_ref,
                     acc_ref, act0_ref, act1_ref, gate_ref, *, n_groups, final_norm):
    s = pl.program_id(0)
    n_items = pl.num_programs(0) - 1
    ec, tt = act0_ref.shape
    nk = PEER_NKEYS
    pack = 2 * V7X_SUBLANES
    assert ec == V7X_SUBLANES * nk

    @pl.when(s == 0)
    def _():
        act0_ref[...] = jnp.zeros_like(act0_ref)
        act1_ref[...] = jnp.zeros_like(act1_ref)

    grp = jnp.clip(s - 1, 0, n_items - 1) % n_groups

    @pl.when(grp == 0)
    def _():
        acc_ref[...] = jnp.zeros_like(acc_ref)

    def gate_tile(act_old, gate_new, c, tls):
        lanes = [slice(tl * V7X_LANES, (tl + 1) * V7X_LANES) for tl in tls]
        w = [[None] * len(lanes) for _ in range(nk // pack)]
        for h in range(PEER_HEADS):
            cnt_b = [jnp.broadcast_to(cnt_ref[h, grp, c:c + 1, ts], (pack, V7X_LANES)).astype(BF16) for ts in lanes]
            e1_b = [jnp.broadcast_to(e1_ref[h, grp, c:c + 1, ts], (pack, V7X_LANES)).astype(BF16) for ts in lanes]
            for rg in range(nk // pack):
                ws = slice(rg * V7X_SUBLANES, (rg + 1) * V7X_SUBLANES)
                for k, ts in enumerate(lanes):
                    e2 = pltpu.bitcast(e2_ref[h, ws, ts], BF16)
                    rank2 = pltpu.bitcast(rank2_ref[h, ws, ts], BF16)
                    term = jnp.where(rank2 < cnt_b[k], e2, jnp.zeros_like(e2)) * e1_b[k]
                    w[rg][k] = term if w[rg][k] is None else w[rg][k] + term
        for rg in range(nk // pack):
            rows = slice(c * nk + rg * pack, c * nk + (rg + 1) * pack)
            for k, ts in enumerate(lanes):
                gate_new[rows, ts] = _gelu_tanh(act_old[rows, ts]).astype(BF16) * w[rg][k]

    def stages(act_new, act_old):
        pair = 2 * V7X_LANES
        for p in range(tt // pair):
            cols = slice(p * pair, (p + 1) * pair)
            act_new[:, cols] = _dot_nt(u_ref[...], xn_ref[cols, :])
            for c in range(V7X_SUBLANES):
                gate_tile(act_old, gate_ref, c, (2 * p, 2 * p + 1))
            acc_ref[:, cols] += _dot(vt_ref[...], gate_ref[:, cols])

    @pl.when(s % 2 == 0)
    def _():
        stages(act0_ref, act1_ref)

    @pl.when(s % 2 == 1)
    def _():
        stages(act1_ref, act0_ref)

    @pl.when((grp == n_groups - 1) & (s >= 1))
    def _():
        out = h2_ref[...] + acc_ref[...].T
        o_ref[...] = _rms(out, fg_ref[...]) if final_norm else out


def _peer_ffn(xn, u, vt, cnt, e1, rank2, e2, h2, f_g, final_norm, tt):
    rows, dm = xn.shape
    tt = min(tt, rows)
    heads, n_keys, _ = cnt.shape
    ec = V7X_SUBLANES * n_keys
    n_groups = u.shape[0] // ec
    n_items = (rows // tt) * n_groups
    item = lambda s, lag: jnp.clip(s - lag, 0, n_items - 1)
    word_spec = pl.BlockSpec((heads, n_keys // 2, tt), lambda s: (0, 0, item(s, 1) // n_groups))
    grp_spec = pl.BlockSpec((heads, n_keys // V7X_SUBLANES, V7X_SUBLANES, tt),
                            lambda s: (0, 0, 0, item(s, 1) // n_groups))
    cnt = cnt.reshape(heads, n_keys // V7X_SUBLANES, V7X_SUBLANES, rows)
    e1 = e1.reshape(heads, n_keys // V7X_SUBLANES, V7X_SUBLANES, rows)
    kern = functools.partial(_peer_ffn_kernel, n_groups=n_groups, final_norm=final_norm)
    return pl.pallas_call(
        kern, grid=(n_items + 1,),
        in_specs=[
            pl.BlockSpec((tt, dm), lambda s: (item(s, 0) // n_groups, 0)),
            pl.BlockSpec((ec, dm), lambda s: (item(s, 0) % n_groups, 0)),
            pl.BlockSpec((dm, ec), lambda s: (0, item(s, 1) % n_groups)),
            grp_spec, grp_spec, word_spec, word_spec,
            pl.BlockSpec((tt, dm), lambda s: (item(s, 1) // n_groups, 0)),
            pl.BlockSpec((1, dm), lambda s: (0, 0)),
        ],
        out_specs=pl.BlockSpec((tt, dm), lambda s: (item(s, 1) // n_groups, 0)),
        out_shape=jax.ShapeDtypeStruct((rows, dm), F32),
        scratch_shapes=[pltpu.VMEM((dm, tt), F32), pltpu.VMEM((ec, tt), F32), pltpu.VMEM((ec, tt), F32),
                        pltpu.VMEM((ec, tt), BF16)],
        compiler_params=_params("arbitrary"), name="peer_ffn",
    )(xn, u, vt, cnt, e1, rank2, e2, h2, f_g)


def _pad_head_cols(w, heads, width):
    lead = w.shape[:-1]
    w = w.reshape(lead + (heads, width))
    w = jnp.pad(w, [(0, 0)] * len(lead) + [(0, 0), (0, V7X_LANES - width)])
    return w.reshape(lead + (heads * V7X_LANES,))


def kernel(x, mem, mix_norm_g, w_in, gla_conv_w, gla_gate_w, gla_gate_b, gla_out_norm_g, sb_out_norm_g, w_out,
           xattn_norm_g, mem_norm_g, w_xq, w_xkv, w_xo, ffn_norm_g, peer_w_q, peer_sub_keys, peer_u, peer_v,
           final_norm_g):
    batch, seq, d = x.shape
    mem_len = mem.shape[1]
    rows = batch * seq
    depth = w_in.shape[0]
    qk = GLA_HEADS * GLA_DK
    vw = GLA_HEADS * GLA_DV
    sbw = SB_HEADS * SB_DH
    c1 = 2 * qk + vw
    c2 = c1 + vw
    c3 = c2 + GLA_LOWRANK
    c4 = c3 + sbw
    c5 = c4 + sbw

    h = x.reshape(rows, d)
    mem2 = mem.reshape(batch * mem_len, d)
    for l in range(depth):
        wl = w_in[l]
        w_conv = jnp.concatenate([_pad_head_cols(wl[:, :qk], GLA_HEADS, GLA_DK),
                                  _pad_head_cols(wl[:, qk:2 * qk], GLA_HEADS, GLA_DK),
                                  wl[:, 2 * qk:c1]], axis=1).astype(BF16)
        w_og = wl[:, c1:c2].astype(BF16)
        w_low = jnp.pad(wl[:, c2:c3], ((0, 0), (0, V7X_LANES - GLA_LOWRANK))).astype(BF16)
        w_sbq, w_sbk, w_sbv = (wl[:, a:b].astype(BF16) for a, b in ((c3, c4), (c4, c5), (c5, c5 + sbw)))
        cw = gla_conv_w[l]
        conv_w = jnp.concatenate([_pad_head_cols(cw[:, :qk], GLA_HEADS, GLA_DK),
                                  _pad_head_cols(cw[:, qk:2 * qk], GLA_HEADS, GLA_DK), cw[:, 2 * qk:]], axis=1)
        gate_w = jnp.pad(_pad_head_cols(gla_gate_w[l], GLA_HEADS, GLA_DK),
                         ((0, V7X_LANES - GLA_LOWRANK), (0, 0))).astype(BF16)
        gate_b = _pad_head_cols(gla_gate_b[l], GLA_HEADS, GLA_DK).reshape(1, -1)

        conv_in, og, low, sbq, sbk, sbv = _norm_matmul(
            h, mix_norm_g[l], [w_conv, w_og, w_low, w_sbq, w_sbk, w_sbv], tm=512)
        o_gla = _gla(conv_in, og, low, conv_w, gate_w, gate_b, gla_out_norm_g[l].reshape(1, -1), batch, seq)
        o_sb = _sb(sbq, sbk, sbv, batch, seq)

        wo = w_out[l].astype(BF16)
        h1, xq = _mix_out(h, o_gla, o_sb, sb_out_norm_g[l].reshape(1, -1), wo[:vw], wo[vw:],
                          xattn_norm_g[l].reshape(1, -1), w_xq[l].astype(BF16), tm=512)
        (kv,) = _norm_matmul(mem2, mem_norm_g[l], [w_xkv[l].astype(BF16)], tm=512)
        keys = peer_sub_keys[l].reshape(PEER_HEADS * 2, PEER_NKEYS, -1).astype(BF16)
        h2, xn, st = _xattn(xq, kv, h1, w_xo[l].astype(BF16), ffn_norm_g[l].reshape(1, -1),
                            peer_w_q[l].astype(BF16), keys, batch, seq, tm=512)
        st = st.reshape(PEER_HEADS, 2, PEER_NKEYS, rows)
        cnt, e1, rank2, e2 = _peer_sel(st, tt=256)
        h = _peer_ffn(xn, peer_u[l].astype(BF16), peer_v[l].T.astype(BF16), cnt, e1, rank2, e2, h2,
                      final_norm_g.reshape(1, -1), final_norm=(l == depth - 1), tt=512)
    return h.reshape(batch, seq, d)
```

```python
import functools

import jax
import jax.numpy as jnp
from jax import lax
from jax.experimental import pallas as pl
from jax.experimental.pallas import tpu as pltpu

EPS = 1e-6
GLA_HEADS = 4
GLA_DK = 64
GLA_DV = 128
GLA_LOWRANK = 16
GLA_GATE_NORM = 16.0
GLA_CHUNK = 128
CONV_WIDTH = 4
SB_HEADS = 8
SB_DH = 64
SB_BLOCK = 256
SB_LOG_WEIGHT_FLOOR = -104.0
X_HEADS = 4
X_DH = 256
PEER_HEADS = 8
PEER_NKEYS = 128
PEER_TOPK = 16

V7X_LANES = 128
V7X_SUBLANES = 8
V7X_VMEM_LIMIT_BYTES = 52 * 1024 * 1024

F32 = jnp.float32
BF16 = jnp.bfloat16
NEG_INF = float("-inf")


def _params(*semantics, flags=None):
    return pltpu.CompilerParams(dimension_semantics=semantics, vmem_limit_bytes=V7X_VMEM_LIMIT_BYTES, flags=flags)


def _dot(a, b):
    return jnp.dot(a, b, preferred_element_type=F32)


def _dot_nt(a, b):
    return lax.dot_general(a, b, (((1,), (1,)), ((), ())), preferred_element_type=F32)


def _dot_tn(a, b):
    return lax.dot_general(a, b, (((0,), (0,)), ((), ())), preferred_element_type=F32)


def _rms(x, g):
    return x * lax.rsqrt(jnp.mean(x * x, axis=-1, keepdims=True) + EPS) * g


def _sigmoid(x):
    return 1.0 / (1.0 + jnp.exp(-x))


def _log_sigmoid(x):
    return jnp.minimum(x, 0.0) - jnp.log(1.0 + jnp.exp(-jnp.abs(x)))


def _split_bf16(x):
    hi = x.astype(BF16)
    lo = (x - hi.astype(F32)).astype(BF16)
    return hi, lo


def _pack_rows(x):
    return pltpu.bitcast(x, jnp.uint32)


def _unpack_rows(words):
    return pltpu.bitcast(words, BF16)


def _norm_matmul_kernel(x_ref, g_ref, *refs):
    n = len(refs) // 2
    xn = _rms(x_ref[...], g_ref[...]).astype(BF16)
    for w_ref, o_ref in zip(refs[:n], refs[n:]):
        o_ref[...] = _dot(xn, w_ref[...]).astype(o_ref.dtype)


def _norm_matmul(x, g, weights, tm):
    rows, d = x.shape
    tm = min(tm, rows)
    in_specs = [pl.BlockSpec((tm, d), lambda i: (i, 0)), pl.BlockSpec((1, d), lambda i: (0, 0))]
    in_specs += [pl.BlockSpec(w.shape, lambda i: (0, 0)) for w in weights]
    out_specs = [pl.BlockSpec((tm, w.shape[1]), lambda i: (i, 0)) for w in weights]
    out_shape = [jax.ShapeDtypeStruct((rows, w.shape[1]), BF16) for w in weights]
    return pl.pallas_call(
        _norm_matmul_kernel, grid=(rows // tm,), in_specs=in_specs, out_specs=out_specs,
        out_shape=out_shape, compiler_params=_params("parallel"), name="norm_matmul",
    )(x, g.reshape(1, d), *weights)


def _gla_kernel(x_ref, og_ref, low_ref, cw_ref, gw_ref, gb_ref, ng_ref, o_ref, halo_ref, st_ref, *, chunk):
    c = chunk
    seq = x_ref.shape[0]
    qk_w = GLA_HEADS * V7X_LANES
    halo_ref[...] = jnp.zeros_like(halo_ref)
    st_ref[...] = jnp.zeros_like(st_ref)
    ri = lax.broadcasted_iota(jnp.int32, (c, c), 0)
    ci = lax.broadcasted_iota(jnp.int32, (c, c), 1)
    causal = ri >= ci
    lmat = jnp.where(causal, 1.0, 0.0).astype(BF16)
    cw = cw_ref[...]
    ng = ng_ref[...]

    def body(n, carry):
        r0 = pl.multiple_of(n * c, c)
        cur = x_ref[pl.ds(r0, c), :].astype(F32)
        ext = jnp.concatenate([halo_ref[...], cur], axis=0)
        y = cur * cw[CONV_WIDTH - 1:CONV_WIDTH, :]
        for j in range(1, CONV_WIDTH):
            tap = cw[CONV_WIDTH - 1 - j:CONV_WIDTH - j, :]
            y = y + pltpu.roll(ext, j, 0)[V7X_SUBLANES:, :] * tap
        halo_ref[...] = cur[c - V7X_SUBLANES:, :]
        y = y * _sigmoid(y)

        gpre = _dot(low_ref[pl.ds(r0, c), :], gw_ref[...]) + gb_ref[...]
        g = _log_sigmoid(gpre) * (1.0 / GLA_GATE_NORM)
        g_hi, g_lo = _split_bf16(g)
        cum = _dot(lmat, g_hi) + _dot(lmat, g_lo)

        for h in range(GLA_HEADS):
            sl = slice(h * V7X_LANES, (h + 1) * V7X_LANES)
            q = y[:, sl] * (GLA_DK ** -0.5)
            k = y[:, qk_w + h * V7X_LANES:qk_w + (h + 1) * V7X_LANES]
            v = y[:, 2 * qk_w + h * GLA_DV:2 * qk_w + (h + 1) * GLA_DV].astype(BF16)
            cm = cum[:, sl]
            last = cm[c - 1:c, :]
            mid = cm[c // 2 - 1:c // 2, :]
            q_in = (q * jnp.exp(cm)).astype(BF16)
            q_mid = (q * jnp.exp(cm - mid)).astype(BF16)
            k_mid = (k * jnp.exp(mid - cm)).astype(BF16)
            k_end = (k * jnp.exp(last - cm)).astype(BF16)
            scores = jnp.where(causal, _dot_nt(q_mid, k_mid), 0.0)
            state = st_ref[h]
            o = _dot(scores.astype(BF16), v) + _dot_nt(q_in, state.astype(BF16))
            st_ref[h] = state * jnp.exp(last) + _dot_tn(v, k_end)
            og = og_ref[pl.ds(r0, c), h * GLA_DV:(h + 1) * GLA_DV].astype(F32)
            o_ref[pl.ds(r0, c), h * GLA_DV:(h + 1) * GLA_DV] = (_rms(o, ng) * (og * _sigmoid(og))).astype(o_ref.dtype)
        return carry

    lax.fori_loop(0, seq // c, body, 0)


def _gla(conv_in, og, low, conv_w, gate_w, gate_b, norm_g, batch, seq):
    w = conv_in.shape[1]
    vw = og.shape[1]
    qk_w = GLA_HEADS * V7X_LANES
    kern = functools.partial(_gla_kernel, chunk=GLA_CHUNK)
    return pl.pallas_call(
        kern, grid=(batch,),
        in_specs=[
            pl.BlockSpec((seq, w), lambda b: (b, 0)),
            pl.BlockSpec((seq, vw), lambda b: (b, 0)),
            pl.BlockSpec((seq, V7X_LANES), lambda b: (b, 0)),
            pl.BlockSpec((CONV_WIDTH, w), lambda b: (0, 0)),
            pl.BlockSpec((V7X_LANES, qk_w), lambda b: (0, 0)),
            pl.BlockSpec((1, qk_w), lambda b: (0, 0)),
            pl.BlockSpec((1, GLA_DV), lambda b: (0, 0)),
        ],
        out_specs=pl.BlockSpec((seq, vw), lambda b: (b, 0)),
        out_shape=jax.ShapeDtypeStruct((batch * seq, vw), BF16),
        scratch_shapes=[pltpu.VMEM((V7X_SUBLANES, w), F32), pltpu.VMEM((GLA_HEADS, GLA_DV, V7X_LANES), F32)],
        compiler_params=_params("parallel"), name="gla",
    )(conv_in, og, low, conv_w, gate_w, gate_b, norm_g)


def _sb_kernel(q_ref, k_ref, v_ref, o_ref, acc_ref, carry_ref, *, blk):
    qi = pl.program_id(2)
    scale = SB_DH ** -0.5
    lane = lax.broadcasted_iota(jnp.int32, (1, V7X_LANES), 1)
    q2 = q_ref[...]
    zero = jnp.zeros_like(q2)
    q_heads = (jnp.where(lane < SB_DH, q2, zero), jnp.where(lane >= SB_DH, q2, zero))
    ri = lax.broadcasted_iota(jnp.int32, (blk, blk), 0)
    ci = lax.broadcasted_iota(jnp.int32, (blk, blk), 1)
    strict = ci < ri
    ri2 = lax.broadcasted_iota(jnp.int32, (blk, 2 * blk), 0)
    ci2 = lax.broadcasted_iota(jnp.int32, (blk, 2 * blk), 1)
    suffix_ones = jnp.where((ri2 > ci2) | (ci2 >= blk), 1.0, 0.0).astype(BF16)
    def block(kb, carry, acc, diagonal=False, keep=None):
        k0 = pl.multiple_of(kb * blk, blk)
        kblk = k_ref[pl.ds(k0, blk), :]
        vblk = v_ref[pl.ds(k0, blk), :]
        new_carry, new_acc = [], []
        for hh in range(2):
            z = _dot_nt(q_heads[hh], kblk) * scale
            soft = jnp.log(1.0 + jnp.exp(-jnp.abs(z)))
            log_beta = jnp.minimum(z, 0.0) - soft
            log_rest = log_beta - z
            if diagonal:
                log_rest = jnp.where(strict, log_rest, 0.0)
            hi, lo = _split_bf16(log_rest)
            sums = _dot(hi, suffix_ones) + _dot(lo, suffix_ones)
            log_w = log_beta + sums[:, :blk]
            w = jnp.exp(log_w if carry[hh] is None else log_w + carry[hh])
            if diagonal:
                w = jnp.where(strict, w, 0.0)
            total = sums[:, blk:]
            if keep is not None:
                w = w * keep
                total = total * keep
            out = _dot(w.astype(BF16), vblk)
            new_acc.append(out if acc[hh] is None else acc[hh] + out)
            new_carry.append(total if carry[hh] is None else carry[hh] + total)
        return new_carry, new_acc

    carry, acc = block(qi, [None, None], [None, None], diagonal=True)
    carry, acc = block(jnp.maximum(qi - 1, 0), carry, acc, keep=(qi > 0).astype(F32))
    for hh in range(2):
        carry_ref[hh] = carry[hh]
        acc_ref[hh] = acc[hh]

    def live():
        return jnp.max(carry_ref[:, :, :V7X_LANES]) > SB_LOG_WEIGHT_FLOOR

    def body(state):
        step, _ = state
        carry, acc = block(qi - 1 - step, [carry_ref[0], carry_ref[1]], [acc_ref[0], acc_ref[1]])
        for hh in range(2):
            carry_ref[hh] = carry[hh]
            acc_ref[hh] = acc[hh]
        return step + 1, live()

    lax.while_loop(lambda state: (state[0] < qi) & state[1], body, (jnp.int32(1), live()))
    o_ref[...] = jnp.where(lane < SB_DH, acc_ref[0], acc_ref[1]).astype(o_ref.dtype)


def _sb(q, k, v, batch, seq):
    blk = min(SB_BLOCK, seq)
    nq = seq // blk
    pairs = SB_HEADS // 2
    kern = functools.partial(_sb_kernel, blk=blk)
    return pl.pallas_call(
        kern, grid=(batch, pairs, nq),
        in_specs=[
            pl.BlockSpec((blk, V7X_LANES), lambda b, g, i: (b * nq + i, g)),
            pl.BlockSpec((seq, V7X_LANES), lambda b, g, i: (b, g)),
            pl.BlockSpec((seq, V7X_LANES), lambda b, g, i: (b, g)),
        ],
        out_specs=pl.BlockSpec((blk, V7X_LANES), lambda b, g, i: (b * nq + i, g)),
        out_shape=jax.ShapeDtypeStruct(q.shape, BF16),
        scratch_shapes=[pltpu.VMEM((2, blk, V7X_LANES), F32), pltpu.VMEM((2, blk, blk), F32)],
        compiler_params=_params("parallel", "parallel", "parallel"), name="stick_breaking",
    )(q, k, v)


def _mix_out_kernel(x_ref, og_ref, osb_ref, sbg_ref, wo_g_ref, wo_s_ref, xg_ref, wq_ref, h1_ref, q_ref):
    osb = _rms(osb_ref[...].astype(F32), sbg_ref[...]).astype(BF16)
    h1 = x_ref[...] + _dot(og_ref[...], wo_g_ref[...]) + _dot(osb, wo_s_ref[...])
    h1_ref[...] = h1
    q_ref[...] = _dot(_rms(h1, xg_ref[...]).astype(BF16), wq_ref[...]).astype(q_ref.dtype)


def _mix_out(x, o_gla, o_sb, sb_g, wo_g, wo_s, x_g, w_xq, tm):
    rows, d = x.shape
    tm = min(tm, rows)
    row = lambda w: pl.BlockSpec((tm, w), lambda i: (i, 0))
    full = lambda a: pl.BlockSpec(a.shape, lambda i: (0, 0))
    return pl.pallas_call(
        _mix_out_kernel, grid=(rows // tm,),
        in_specs=[row(d), row(o_gla.shape[1]), row(o_sb.shape[1]), full(sb_g), full(wo_g), full(wo_s), full(x_g), full(w_xq)],
        out_specs=[row(d), row(w_xq.shape[1])],
        out_shape=[jax.ShapeDtypeStruct((rows, d), F32), jax.ShapeDtypeStruct((rows, w_xq.shape[1]), BF16)],
        compiler_params=_params("parallel"), name="mix_out",
    )(x, o_gla, o_sb, sb_g, wo_g, wo_s, x_g, w_xq)


def _xattn_kernel(q_ref, kv_ref, h1_ref, wo_ref, fg_ref, wpq_ref, keys_ref, h2_ref, xn_ref, st_ref):
    xw = X_HEADS * X_DH
    outs = []
    for h in range(X_HEADS):
        qh = q_ref[:, h * X_DH:(h + 1) * X_DH]
        kh = kv_ref[:, h * X_DH:(h + 1) * X_DH]
        vh = kv_ref[:, xw + h * X_DH:xw + (h + 1) * X_DH]
        s = _dot_nt(qh, kh) * (X_DH ** -0.5)
        p = jnp.exp(s - jnp.max(s, axis=-1, keepdims=True))
        o = _dot(p.astype(BF16), vh) / jnp.sum(p, axis=-1, keepdims=True)
        outs.append(o.astype(BF16))
    h2 = h1_ref[...] + _dot(jnp.concatenate(outs, axis=-1), wo_ref[...])
    h2_ref[...] = h2
    xn = _rms(h2, fg_ref[...]).astype(BF16)
    xn_ref[...] = _pack_rows(xn)
    pq = _dot(xn, wpq_ref[...])
    half = keys_ref.shape[2]
    for hp in range(keys_ref.shape[0]):
        st_ref[hp] = _dot_nt(keys_ref[hp], pq[:, hp * half:(hp + 1) * half].astype(BF16))


def _xattn(q, kv, h1, w_xo, f_g, w_pq, keys, batch, seq, tm):
    rows, d = h1.shape
    tm = min(tm, seq)
    nt = seq // tm
    mem_len = kv.shape[0] // batch
    row = lambda w: pl.BlockSpec((tm, w), lambda b, i: (b * nt + i, 0))
    full = lambda a: pl.BlockSpec(a.shape, lambda b, i: (0,) * a.ndim)
    n_hp, n_keys, _ = keys.shape
    return pl.pallas_call(
        _xattn_kernel, grid=(batch, nt),
        in_specs=[row(q.shape[1]), pl.BlockSpec((mem_len, kv.shape[1]), lambda b, i: (b, 0)), row(d),
                  full(w_xo), full(f_g), full(w_pq), full(keys)],
        out_specs=[row(d), pl.BlockSpec((tm // 2, d), lambda b, i: (b * nt + i, 0)),
                   pl.BlockSpec((n_hp, n_keys, tm), lambda b, i: (0, 0, b * nt + i))],
        out_shape=[jax.ShapeDtypeStruct((rows, d), F32), jax.ShapeDtypeStruct((rows // 2, d), jnp.uint32),
                   jax.ShapeDtypeStruct((n_hp, n_keys, rows), F32)],
        compiler_params=_params("parallel", "parallel"), name="xattn_peer_scores",
    )(q, kv, h1, w_xo, f_g, w_pq, keys)


def _odd_even_merge_sort(n):
    pairs = []

    def merge(lo, length, stride):
        step = stride * 2
        if step < length:
            merge(lo, length, step)
            merge(lo + stride, length, step)
            pairs.extend((i, i + stride) for i in range(lo + stride, lo + length - stride, step))
        else:
            pairs.append((lo, lo + stride))

    def sort(lo, length):
        if length > 1:
            sort(lo, length // 2)
            sort(lo + length // 2, length // 2)
            merge(lo, length, 1)

    sort(0, n)
    return pairs


_SORT16 = _odd_even_merge_sort(PEER_NKEYS // V7X_SUBLANES)


def _peer_sel_kernel(st_ref, cnt_ref, e1_ref, rank2_ref, e2_ref):
    n_top = PEER_TOPK + 1
    not_ranked = float(PEER_NKEYS - 1)
    n_slabs = PEER_NKEYS // V7X_SUBLANES

    def top_values(x):
        cols = [x[v * V7X_SUBLANES:(v + 1) * V7X_SUBLANES, :] for v in range(n_slabs)]
        for i, j in _SORT16:
            cols[i], cols[j] = jnp.maximum(cols[i], cols[j]), jnp.minimum(cols[i], cols[j])
        cols.append(jnp.full(cols[0].shape, NEG_INF, F32))
        vals = []
        for r in range(n_top):
            m = jnp.max(cols[0], axis=0, keepdims=True)
            vals.append(m)
            hit = cols[0] == m
            for k in range(min(n_slabs, n_top - 1 - r)):
                cols[k] = jnp.where(hit, cols[k + 1], cols[k])
        return vals

    tops = [[top_values(st_ref[h, p]) for h in range(PEER_HEADS)] for p in range(2)]
    packed = [[jnp.concatenate([tops[p][h][r] for h in range(PEER_HEADS)], axis=0) for r in range(n_top)]
              for p in range(2)]
    pairs = [(i, j) for i in range(n_top) for j in range(n_top) if (i + 1) * (j + 1) <= n_top]
    sums = {ij: packed[0][ij[0]] + packed[1][ij[1]] for ij in pairs}
    cands = [sums[ij] for ij in pairs]
    best = []
    for _ in range(n_top):
        m = functools.reduce(jnp.maximum, cands)
        best.append(m)
        cands = [jnp.where(c == m, NEG_INF, c) for c in cands]
    thr = 0.5 * (best[PEER_TOPK - 1] + best[PEER_TOPK])
    z = functools.reduce(lambda a, b: a + b, [jnp.exp(b - best[0]) for b in best[:PEER_TOPK]])
    inv_z = 1.0 / z
    cnt_by_rank = [functools.reduce(lambda a, b: a + b,
                                    [jnp.where(sums[(i, j)] >= thr, 1.0, 0.0) for j in range(n_top) if (i, j) in sums])
                   for i in range(PEER_TOPK)]
    at_least = [functools.reduce(jnp.minimum, [jnp.where(cnt_by_rank[i] >= float(k), packed[0][i], -NEG_INF)
                                               for i in range(PEER_TOPK)])
                for k in range(1, PEER_TOPK + 1)]
    for h in range(PEER_HEADS):
        s1 = st_ref[h, 0]
        s2 = st_ref[h, 1]
        cnt = jnp.zeros(s1.shape, F32)
        for k in range(1, PEER_TOPK + 1):
            cnt = jnp.where(s1 >= at_least[k - 1][h:h + 1, :], float(k), cnt)
        cnt_ref[h] = cnt
        e1_ref[h] = jnp.exp(s1 - tops[0][h][0]) * inv_z[h:h + 1, :]
        rank2 = jnp.full(s2.shape, not_ranked, F32)
        for r in reversed(range(n_top)):
            rank2 = jnp.where(s2 >= tops[1][h][r], float(r), rank2)
        rank2_ref[h] = pltpu.bitcast(rank2.astype(BF16), jnp.uint32)
        e2_ref[h] = pltpu.bitcast(jnp.exp(s2 - tops[1][h][0]).astype(BF16), jnp.uint32)


def _peer_sel(st, tt):
    heads, _, n_keys, rows = st.shape
    tt = min(tt, rows)
    spec = pl.BlockSpec((heads, n_keys, tt), lambda i: (0, 0, i))
    word_spec = pl.BlockSpec((heads, n_keys // 2, tt), lambda i: (0, 0, i))
    f32_shape = jax.ShapeDtypeStruct((heads, n_keys, rows), F32)
    word_shape = jax.ShapeDtypeStruct((heads, n_keys // 2, rows), jnp.uint32)
    return pl.pallas_call(
        _peer_sel_kernel, grid=(rows // tt,),
        in_specs=[pl.BlockSpec((heads, 2, n_keys, tt), lambda i: (0, 0, 0, i))],
        out_specs=[spec, spec, word_spec, word_spec], out_shape=[f32_shape, f32_shape, word_shape, word_shape],
        compiler_params=_params("parallel"), name="peer_select",
    )(st)


def _gelu_tanh(x):
    k = 2.0 * 0.7978845608028654 * 1.4426950408889634
    return x / (1.0 + jnp.exp2(x * (-k - (k * 0.044715) * (x * x))))


def _peer_ffn_kernel(xn_ref, u_ref, vt_ref, cnt_ref, e1_ref, rank2_ref, e2_ref, h2_ref, fg_ref, o_ref,
                     acc_ref, act0_ref, act1_ref, gate_ref, *, n_groups, final_norm):
    s = pl.program_id(0)
    n_items = pl.num_programs(0) - 1
    ec, tt = act0_ref.shape
    nk = PEER_NKEYS
    pack = 2 * V7X_SUBLANES
    assert ec == V7X_SUBLANES * nk

    @pl.when(s == 0)
    def _():
        act0_ref[...] = jnp.zeros_like(act0_ref)
        act1_ref[...] = jnp.zeros_like(act1_ref)

    grp = jnp.clip(s - 1, 0, n_items - 1) % n_groups

    @pl.when(grp == 0)
    def _():
        acc_ref[...] = jnp.zeros_like(acc_ref)

    def gate_tile(act_old, gate_new, c, tls):
        lanes = [slice(tl * V7X_LANES, (tl + 1) * V7X_LANES) for tl in tls]
        w = [[None] * len(lanes) for _ in range(nk // pack)]
        for h in range(PEER_HEADS):
            cnt_b = [jnp.broadcast_to(cnt_ref[h, grp, c:c + 1, ts], (pack, V7X_LANES)).astype(BF16) for ts in lanes]
            e1_b = [jnp.broadcast_to(e1_ref[h, grp, c:c + 1, ts], (pack, V7X_LANES)).astype(BF16) for ts in lanes]
            for rg in range(nk // pack):
                ws = slice(rg * V7X_SUBLANES, (rg + 1) * V7X_SUBLANES)
                for k, ts in enumerate(lanes):
                    e2 = pltpu.bitcast(e2_ref[h, ws, ts], BF16)
                    rank2 = pltpu.bitcast(rank2_ref[h, ws, ts], BF16)
                    term = jnp.where(rank2 < cnt_b[k], e2, jnp.zeros_like(e2)) * e1_b[k]
                    w[rg][k] = term if w[rg][k] is None else w[rg][k] + term
        for rg in range(nk // pack):
            rows = slice(c * nk + rg * pack, c * nk + (rg + 1) * pack)
            words = slice((c * nk + rg * pack) // 2, (c * nk + (rg + 1) * pack) // 2)
            for k, ts in enumerate(lanes):
                gate_new[words, ts] = _pack_rows(_gelu_tanh(act_old[rows, ts]).astype(BF16) * w[rg][k])

    def stages(act_new, act_old):
        pair = 2 * V7X_LANES
        for p in range(tt // pair):
            cols = slice(p * pair, (p + 1) * pair)
            xn = _unpack_rows(xn_ref[p * pair // 2:(p + 1) * pair // 2, :])
            act_new[:, cols] = _dot_nt(_unpack_rows(u_ref[...]), xn)
            for c in range(V7X_SUBLANES):
                gate_tile(act_old, gate_ref, c, (2 * p, 2 * p + 1))
            acc_ref[:, cols] += _dot(_unpack_rows(vt_ref[...]), _unpack_rows(gate_ref[:, cols]))

    @pl.when(s % 2 == 0)
    def _():
        stages(act0_ref, act1_ref)

    @pl.when(s % 2 == 1)
    def _():
        stages(act1_ref, act0_ref)

    @pl.when((grp == n_groups - 1) & (s >= 1))
    def _():
        out = h2_ref[...] + acc_ref[...].T
        o_ref[...] = _rms(out, fg_ref[...]) if final_norm else out


def _peer_prep_kernel(u_ref, v_ref, uw_ref, vtw_ref):
    uw_ref[...] = _pack_rows(u_ref[...].astype(BF16))
    vtw_ref[...] = _pack_rows(v_ref[...].T.astype(BF16))


def _peer_prep(u, v, ec):
    n, d = u.shape
    return pl.pallas_call(
        _peer_prep_kernel, grid=(n // ec,),
        in_specs=[pl.BlockSpec((ec, d), lambda i: (i, 0)), pl.BlockSpec((ec, d), lambda i: (i, 0))],
        out_specs=[pl.BlockSpec((ec // 2, d), lambda i: (i, 0)), pl.BlockSpec((d // 2, ec), lambda i: (0, i))],
        out_shape=[jax.ShapeDtypeStruct((n // 2, d), jnp.uint32), jax.ShapeDtypeStruct((d // 2, n), jnp.uint32)],
        compiler_params=_params("parallel"), name="peer_prep",
    )(u, v)


def _peer_ffn(xn, u, vt, cnt, e1, rank2, e2, h2, f_g, final_norm, tt):
    rows, dm = h2.shape
    tt = min(tt, rows)
    heads, n_keys, _ = cnt.shape
    ec = V7X_SUBLANES * n_keys
    n_groups = 2 * u.shape[0] // ec
    n_items = (rows // tt) * n_groups
    item = lambda s, lag: jnp.clip(s - lag, 0, n_items - 1)
    word_spec = pl.BlockSpec((heads, n_keys // 2, tt), lambda s: (0, 0, item(s, 1) // n_groups))
    grp_spec = pl.BlockSpec((heads, n_keys // V7X_SUBLANES, V7X_SUBLANES, tt),
                            lambda s: (0, 0, 0, item(s, 1) // n_groups))
    cnt = cnt.reshape(heads, n_keys // V7X_SUBLANES, V7X_SUBLANES, rows)
    e1 = e1.reshape(heads, n_keys // V7X_SUBLANES, V7X_SUBLANES, rows)
    kern = functools.partial(_peer_ffn_kernel, n_groups=n_groups, final_norm=final_norm)
    return pl.pallas_call(
        kern, grid=(n_items + 1,),
        in_specs=[
            pl.BlockSpec((tt // 2, dm), lambda s: (item(s, 0) // n_groups, 0)),
            pl.BlockSpec((ec // 2, dm), lambda s: (item(s, 0) % n_groups, 0)),
            pl.BlockSpec((dm // 2, ec), lambda s: (0, item(s, 1) % n_groups)),
            grp_spec, grp_spec, word_spec, word_spec,
            pl.BlockSpec((tt, dm), lambda s: (item(s, 1) // n_groups, 0)),
            pl.BlockSpec((1, dm), lambda s: (0, 0)),
        ],
        out_specs=pl.BlockSpec((tt, dm), lambda s: (item(s, 1) // n_groups, 0)),
        out_shape=jax.ShapeDtypeStruct((rows, dm), F32),
        scratch_shapes=[pltpu.VMEM((dm, tt), F32), pltpu.VMEM((ec, tt), F32), pltpu.VMEM((ec, tt), F32),
                        pltpu.VMEM((ec // 2, tt), jnp.uint32)],
        compiler_params=_params("arbitrary"), name="peer_ffn",
    )(xn, u, vt, cnt, e1, rank2, e2, h2, f_g)


def _pad_head_cols(w, heads, width):
    lead = w.shape[:-1]
    w = w.reshape(lead + (heads, width))
    w = jnp.pad(w, [(0, 0)] * len(lead) + [(0, 0), (0, V7X_LANES - width)])
    return w.reshape(lead + (heads * V7X_LANES,))


def kernel(x, mem, mix_norm_g, w_in, gla_conv_w, gla_gate_w, gla_gate_b, gla_out_norm_g, sb_out_norm_g, w_out,
           xattn_norm_g, mem_norm_g, w_xq, w_xkv, w_xo, ffn_norm_g, peer_w_q, peer_sub_keys, peer_u, peer_v,
           final_norm_g):
    batch, seq, d = x.shape
    mem_len = mem.shape[1]
    rows = batch * seq
    depth = w_in.shape[0]
    qk = GLA_HEADS * GLA_DK
    vw = GLA_HEADS * GLA_DV
    sbw = SB_HEADS * SB_DH
    c1 = 2 * qk + vw
    c2 = c1 + vw
    c3 = c2 + GLA_LOWRANK
    c4 = c3 + sbw
    c5 = c4 + sbw

    h = x.reshape(rows, d)
    mem2 = mem.reshape(batch * mem_len, d)
    for l in range(depth):
        wl = w_in[l]
        w_conv = jnp.concatenate([_pad_head_cols(wl[:, :qk], GLA_HEADS, GLA_DK),
                                  _pad_head_cols(wl[:, qk:2 * qk], GLA_HEADS, GLA_DK),
                                  wl[:, 2 * qk:c1]], axis=1).astype(BF16)
        w_og = wl[:, c1:c2].astype(BF16)
        w_low = jnp.pad(wl[:, c2:c3], ((0, 0), (0, V7X_LANES - GLA_LOWRANK))).astype(BF16)
        w_sbq, w_sbk, w_sbv = (wl[:, a:b].astype(BF16) for a, b in ((c3, c4), (c4, c5), (c5, c5 + sbw)))
        cw = gla_conv_w[l]
        conv_w = jnp.concatenate([_pad_head_cols(cw[:, :qk], GLA_HEADS, GLA_DK),
                                  _pad_head_cols(cw[:, qk:2 * qk], GLA_HEADS, GLA_DK), cw[:, 2 * qk:]], axis=1)
        gate_w = jnp.pad(_pad_head_cols(gla_gate_w[l], GLA_HEADS, GLA_DK),
                         ((0, V7X_LANES - GLA_LOWRANK), (0, 0))).astype(BF16)
        gate_b = _pad_head_cols(gla_gate_b[l], GLA_HEADS, GLA_DK).reshape(1, -1)

        conv_in, og, low, sbq, sbk, sbv = _norm_matmul(
            h, mix_norm_g[l], [w_conv, w_og, w_low, w_sbq, w_sbk, w_sbv], tm=512)
        o_gla = _gla(conv_in, og, low, conv_w, gate_w, gate_b, gla_out_norm_g[l].reshape(1, -1), batch, seq)
        o_sb = _sb(sbq, sbk, sbv, batch, seq)

        wo = w_out[l].astype(BF16)
        h1, xq = _mix_out(h, o_gla, o_sb, sb_out_norm_g[l].reshape(1, -1), wo[:vw], wo[vw:],
                          xattn_norm_g[l].reshape(1, -1), w_xq[l].astype(BF16), tm=512)
        (kv,) = _norm_matmul(mem2, mem_norm_g[l], [w_xkv[l].astype(BF16)], tm=512)
        keys = peer_sub_keys[l].reshape(PEER_HEADS * 2, PEER_NKEYS, -1).astype(BF16)
        h2, xn, st = _xattn(xq, kv, h1, w_xo[l].astype(BF16), ffn_norm_g[l].reshape(1, -1),
                            peer_w_q[l].astype(BF16), keys, batch, seq, tm=512)
        st = st.reshape(PEER_HEADS, 2, PEER_NKEYS, rows)
        cnt, e1, rank2, e2 = _peer_sel(st, tt=256)
        u_words, vt_words = _peer_prep(peer_u[l], peer_v[l], ec=V7X_SUBLANES * PEER_NKEYS)
        h = _peer_ffn(xn, u_words, vt_words, cnt, e1, rank2, e2, h2,
                      final_norm_g.reshape(1, -1), final_norm=(l == depth - 1), tt=512)
    return h.reshape(batch, seq, d)
```

```python
import functools

import jax
import jax.numpy as jnp
from jax import lax
from jax.experimental import pallas as pl
from jax.experimental.pallas import tpu as pltpu

EPS = 1e-6
GLA_HEADS = 4
GLA_DK = 64
GLA_DV = 128
GLA_LOWRANK = 16
GLA_GATE_NORM = 16.0
GLA_CHUNK = 128
CONV_WIDTH = 4
SB_HEADS = 8
SB_DH = 64
SB_BLOCK = 256
SB_LOG_WEIGHT_FLOOR = -104.0
X_HEADS = 4
X_DH = 256
PEER_HEADS = 8
PEER_NKEYS = 128
PEER_TOPK = 16

V7X_LANES = 128
V7X_SUBLANES = 8
V7X_VMEM_LIMIT_BYTES = 52 * 1024 * 1024

F32 = jnp.float32
BF16 = jnp.bfloat16
NEG_INF = float("-inf")


def _params(*semantics, flags=None):
    return pltpu.CompilerParams(dimension_semantics=semantics, vmem_limit_bytes=V7X_VMEM_LIMIT_BYTES, flags=flags)


def _dot(a, b):
    return jnp.dot(a, b, preferred_element_type=F32)


def _dot_nt(a, b):
    return lax.dot_general(a, b, (((1,), (1,)), ((), ())), preferred_element_type=F32)


def _dot_tn(a, b):
    return lax.dot_general(a, b, (((0,), (0,)), ((), ())), preferred_element_type=F32)


def _rms(x, g):
    return x * lax.rsqrt(jnp.mean(x * x, axis=-1, keepdims=True) + EPS) * g


def _sigmoid(x):
    return 1.0 / (1.0 + jnp.exp(-x))


def _log_sigmoid(x):
    return jnp.minimum(x, 0.0) - jnp.log(1.0 + jnp.exp(-jnp.abs(x)))


def _split_bf16(x):
    hi = x.astype(BF16)
    lo = (x - hi.astype(F32)).astype(BF16)
    return hi, lo


def _pack_rows(x):
    return pltpu.bitcast(x, jnp.uint32)


def _unpack_rows(words):
    return pltpu.bitcast(words, BF16)


def _norm_matmul_kernel(x_ref, g_ref, *refs):
    n = len(refs) // 2
    xn = _rms(x_ref[...], g_ref[...]).astype(BF16)
    for w_ref, o_ref in zip(refs[:n], refs[n:]):
        o_ref[...] = _dot(xn, w_ref[...]).astype(o_ref.dtype)


def _norm_matmul(x, g, weights, tm):
    rows, d = x.shape
    tm = min(tm, rows)
    in_specs = [pl.BlockSpec((tm, d), lambda i: (i, 0)), pl.BlockSpec((1, d), lambda i: (0, 0))]
    in_specs += [pl.BlockSpec(w.shape, lambda i: (0, 0)) for w in weights]
    out_specs = [pl.BlockSpec((tm, w.shape[1]), lambda i: (i, 0)) for w in weights]
    out_shape = [jax.ShapeDtypeStruct((rows, w.shape[1]), BF16) for w in weights]
    return pl.pallas_call(
        _norm_matmul_kernel, grid=(rows // tm,), in_specs=in_specs, out_specs=out_specs,
        out_shape=out_shape, compiler_params=_params("parallel"), name="norm_matmul",
    )(x, g.reshape(1, d), *weights)


def _gla_kernel(x_ref, og_ref, low_ref, cw_ref, gw_ref, gb_ref, ng_ref, o_ref, halo_ref, st_ref, *, chunk):
    c = chunk
    seq = x_ref.shape[0]
    qk_w = GLA_HEADS * V7X_LANES
    halo_ref[...] = jnp.zeros_like(halo_ref)
    st_ref[...] = jnp.zeros_like(st_ref)
    ri = lax.broadcasted_iota(jnp.int32, (c, c), 0)
    ci = lax.broadcasted_iota(jnp.int32, (c, c), 1)
    causal = ri >= ci
    lmat = jnp.where(causal, 1.0, 0.0).astype(BF16)
    cw = cw_ref[...]
    ng = ng_ref[...]

    def body(n, carry):
        r0 = pl.multiple_of(n * c, c)
        cur = x_ref[pl.ds(r0, c), :].astype(F32)
        ext = jnp.concatenate([halo_ref[...], cur], axis=0)
        y = cur * cw[CONV_WIDTH - 1:CONV_WIDTH, :]
        for j in range(1, CONV_WIDTH):
            tap = cw[CONV_WIDTH - 1 - j:CONV_WIDTH - j, :]
            y = y + pltpu.roll(ext, j, 0)[V7X_SUBLANES:, :] * tap
        halo_ref[...] = cur[c - V7X_SUBLANES:, :]
        y = y * _sigmoid(y)

        gpre = _dot(low_ref[pl.ds(r0, c), :], gw_ref[...]) + gb_ref[...]
        g = _log_sigmoid(gpre) * (1.0 / GLA_GATE_NORM)
        g_hi, g_lo = _split_bf16(g)
        cum = _dot(lmat, g_hi) + _dot(lmat, g_lo)

        for h in range(GLA_HEADS):
            sl = slice(h * V7X_LANES, (h + 1) * V7X_LANES)
            q = y[:, sl] * (GLA_DK ** -0.5)
            k = y[:, qk_w + h * V7X_LANES:qk_w + (h + 1) * V7X_LANES]
            v = y[:, 2 * qk_w + h * GLA_DV:2 * qk_w + (h + 1) * GLA_DV].astype(BF16)
            cm = cum[:, sl]
            last = cm[c - 1:c, :]
            mid = cm[c // 2 - 1:c // 2, :]
            q_in = (q * jnp.exp(cm)).astype(BF16)
            q_mid = (q * jnp.exp(cm - mid)).astype(BF16)
            k_mid = (k * jnp.exp(mid - cm)).astype(BF16)
            k_end = (k * jnp.exp(last - cm)).astype(BF16)
            scores = jnp.where(causal, _dot_nt(q_mid, k_mid), 0.0)
            state = st_ref[h]
            o = _dot(scores.astype(BF16), v) + _dot_nt(q_in, state.astype(BF16))
            st_ref[h] = state * jnp.exp(last) + _dot_tn(v, k_end)
            og = og_ref[pl.ds(r0, c), h * GLA_DV:(h + 1) * GLA_DV].astype(F32)
            o_ref[pl.ds(r0, c), h * GLA_DV:(h + 1) * GLA_DV] = (_rms(o, ng) * (og * _sigmoid(og))).astype(o_ref.dtype)
        return carry

    lax.fori_loop(0, seq // c, body, 0)


def _gla(conv_in, og, low, conv_w, gate_w, gate_b, norm_g, batch, seq):
    w = conv_in.shape[1]
    vw = og.shape[1]
    qk_w = GLA_HEADS * V7X_LANES
    kern = functools.partial(_gla_kernel, chunk=GLA_CHUNK)
    return pl.pallas_call(
        kern, grid=(batch,),
        in_specs=[
            pl.BlockSpec((seq, w), lambda b: (b, 0)),
            pl.BlockSpec((seq, vw), lambda b: (b, 0)),
            pl.BlockSpec((seq, V7X_LANES), lambda b: (b, 0)),
            pl.BlockSpec((CONV_WIDTH, w), lambda b: (0, 0)),
            pl.BlockSpec((V7X_LANES, qk_w), lambda b: (0, 0)),
            pl.BlockSpec((1, qk_w), lambda b: (0, 0)),
            pl.BlockSpec((1, GLA_DV), lambda b: (0, 0)),
        ],
        out_specs=pl.BlockSpec((seq, vw), lambda b: (b, 0)),
        out_shape=jax.ShapeDtypeStruct((batch * seq, vw), BF16),
        scratch_shapes=[pltpu.VMEM((V7X_SUBLANES, w), F32), pltpu.VMEM((GLA_HEADS, GLA_DV, V7X_LANES), F32)],
        compiler_params=_params("parallel"), name="gla",
    )(conv_in, og, low, conv_w, gate_w, gate_b, norm_g)


def _sb_kernel(q_ref, k_ref, v_ref, o_ref, acc_ref, carry_ref, *, blk):
    qi = pl.program_id(2)
    scale = SB_DH ** -0.5
    lane = lax.broadcasted_iota(jnp.int32, (1, V7X_LANES), 1)
    q2 = q_ref[...]
    zero = jnp.zeros_like(q2)
    q_heads = (jnp.where(lane < SB_DH, q2, zero), jnp.where(lane >= SB_DH, q2, zero))
    ri = lax.broadcasted_iota(jnp.int32, (blk, blk), 0)
    ci = lax.broadcasted_iota(jnp.int32, (blk, blk), 1)
    strict = ci < ri
    rj = lax.broadcasted_iota(jnp.int32, (2 * blk, blk), 0) % blk
    cs = lax.broadcasted_iota(jnp.int32, (2 * blk, blk), 1)
    suffix = jnp.where(rj > cs, 1.0, 0.0).astype(BF16)
    def block(kb, carry, acc, diagonal=False, keep=None):
        k0 = pl.multiple_of(kb * blk, blk)
        kblk = k_ref[pl.ds(k0, blk), :]
        vblk = v_ref[pl.ds(k0, blk), :]
        new_carry, new_acc = [], []
        for hh in range(2):
            z = _dot_nt(q_heads[hh], kblk) * scale
            soft = jnp.log(1.0 + jnp.exp(-jnp.abs(z)))
            log_beta = jnp.minimum(z, 0.0) - soft
            log_rest = log_beta - z
            if diagonal:
                log_rest = jnp.where(strict, log_rest, 0.0)
            hi, lo = _split_bf16(log_rest)
            tail = _dot(jnp.concatenate([hi, lo], axis=1), suffix)
            log_w = log_beta + tail
            w = jnp.exp(log_w if carry[hh] is None else log_w + carry[hh])
            if diagonal:
                w = jnp.where(strict, w, 0.0)
            total = jnp.broadcast_to(log_rest[:, 0:1] + tail[:, 0:1], (blk, blk))
            if keep is not None:
                w = w * keep
                total = total * keep
            out = _dot(w.astype(BF16), vblk)
            new_acc.append(out if acc[hh] is None else acc[hh] + out)
            new_carry.append(total if carry[hh] is None else carry[hh] + total)
        return new_carry, new_acc

    carry, acc = block(qi, [None, None], [None, None], diagonal=True)
    carry, acc = block(jnp.maximum(qi - 1, 0), carry, acc, keep=(qi > 0).astype(F32))
    for hh in range(2):
        carry_ref[hh] = carry[hh]
        acc_ref[hh] = acc[hh]

    def live():
        return jnp.max(carry_ref[:, :, :V7X_LANES]) > SB_LOG_WEIGHT_FLOOR

    def body(state):
        step, _ = state
        carry, acc = block(qi - 1 - step, [carry_ref[0], carry_ref[1]], [acc_ref[0], acc_ref[1]])
        for hh in range(2):
            carry_ref[hh] = carry[hh]
            acc_ref[hh] = acc[hh]
        return step + 1, live()

    lax.while_loop(lambda state: (state[0] < qi) & state[1], body, (jnp.int32(1), live()))
    o_ref[...] = jnp.where(lane < SB_DH, acc_ref[0], acc_ref[1]).astype(o_ref.dtype)


def _sb(q, k, v, batch, seq):
    blk = min(SB_BLOCK, seq)
    nq = seq // blk
    pairs = SB_HEADS // 2
    kern = functools.partial(_sb_kernel, blk=blk)
    return pl.pallas_call(
        kern, grid=(batch, pairs, nq),
        in_specs=[
            pl.BlockSpec((blk, V7X_LANES), lambda b, g, i: (b * nq + i, g)),
            pl.BlockSpec((seq, V7X_LANES), lambda b, g, i: (b, g)),
            pl.BlockSpec((seq, V7X_LANES), lambda b, g, i: (b, g)),
        ],
        out_specs=pl.BlockSpec((blk, V7X_LANES), lambda b, g, i: (b * nq + i, g)),
        out_shape=jax.ShapeDtypeStruct(q.shape, BF16),
        scratch_shapes=[pltpu.VMEM((2, blk, V7X_LANES), F32), pltpu.VMEM((2, blk, blk), F32)],
        compiler_params=_params("parallel", "parallel", "parallel"), name="stick_breaking",
    )(q, k, v)


def _mix_out_kernel(x_ref, og_ref, osb_ref, sbg_ref, wo_g_ref, wo_s_ref, xg_ref, wq_ref, h1_ref, q_ref):
    osb = _rms(osb_ref[...].astype(F32), sbg_ref[...]).astype(BF16)
    h1 = x_ref[...] + _dot(og_ref[...], wo_g_ref[...]) + _dot(osb, wo_s_ref[...])
    h1_ref[...] = h1
    q_ref[...] = _dot(_rms(h1, xg_ref[...]).astype(BF16), wq_ref[...]).astype(q_ref.dtype)


def _mix_out(x, o_gla, o_sb, sb_g, wo_g, wo_s, x_g, w_xq, tm):
    rows, d = x.shape
    tm = min(tm, rows)
    row = lambda w: pl.BlockSpec((tm, w), lambda i: (i, 0))
    full = lambda a: pl.BlockSpec(a.shape, lambda i: (0, 0))
    return pl.pallas_call(
        _mix_out_kernel, grid=(rows // tm,),
        in_specs=[row(d), row(o_gla.shape[1]), row(o_sb.shape[1]), full(sb_g), full(wo_g), full(wo_s), full(x_g), full(w_xq)],
        out_specs=[row(d), row(w_xq.shape[1])],
        out_shape=[jax.ShapeDtypeStruct((rows, d), F32), jax.ShapeDtypeStruct((rows, w_xq.shape[1]), BF16)],
        compiler_params=_params("parallel"), name="mix_out",
    )(x, o_gla, o_sb, sb_g, wo_g, wo_s, x_g, w_xq)


def _xattn_kernel(q_ref, kv_ref, h1_ref, wo_ref, fg_ref, wpq_ref, keys_ref, h2_ref, xn_ref, st_ref):
    xw = X_HEADS * X_DH
    outs = []
    for h in range(X_HEADS):
        qh = q_ref[:, h * X_DH:(h + 1) * X_DH]
        kh = kv_ref[:, h * X_DH:(h + 1) * X_DH]
        vh = kv_ref[:, xw + h * X_DH:xw + (h + 1) * X_DH]
        s = _dot_nt(qh, kh) * (X_DH ** -0.5)
        p = jnp.exp(s - jnp.max(s, axis=-1, keepdims=True))
        o = _dot(p.astype(BF16), vh) / jnp.sum(p, axis=-1, keepdims=True)
        outs.append(o.astype(BF16))
    h2 = h1_ref[...] + _dot(jnp.concatenate(outs, axis=-1), wo_ref[...])
    h2_ref[...] = h2
    xn = _rms(h2, fg_ref[...]).astype(BF16)
    xn_ref[...] = _pack_rows(xn)
    pq = _dot(xn, wpq_ref[...])
    half = keys_ref.shape[2]
    for hp in range(keys_ref.shape[0]):
        st_ref[hp] = _dot_nt(keys_ref[hp], pq[:, hp * half:(hp + 1) * half].astype(BF16))


def _xattn(q, kv, h1, w_xo, f_g, w_pq, keys, batch, seq, tm):
    rows, d = h1.shape
    tm = min(tm, seq)
    nt = seq // tm
    mem_len = kv.shape[0] // batch
    row = lambda w: pl.BlockSpec((tm, w), lambda b, i: (b * nt + i, 0))
    full = lambda a: pl.BlockSpec(a.shape, lambda b, i: (0,) * a.ndim)
    n_hp, n_keys, _ = keys.shape
    return pl.pallas_call(
        _xattn_kernel, grid=(batch, nt),
        in_specs=[row(q.shape[1]), pl.BlockSpec((mem_len, kv.shape[1]), lambda b, i: (b, 0)), row(d),
                  full(w_xo), full(f_g), full(w_pq), full(keys)],
        out_specs=[row(d), pl.BlockSpec((tm // 2, d), lambda b, i: (b * nt + i, 0)),
                   pl.BlockSpec((n_hp, n_keys, tm), lambda b, i: (0, 0, b * nt + i))],
        out_shape=[jax.ShapeDtypeStruct((rows, d), F32), jax.ShapeDtypeStruct((rows // 2, d), jnp.uint32),
                   jax.ShapeDtypeStruct((n_hp, n_keys, rows), F32)],
        compiler_params=_params("parallel", "parallel"), name="xattn_peer_scores",
    )(q, kv, h1, w_xo, f_g, w_pq, keys)


def _odd_even_merge_sort(n):
    pairs = []

    def merge(lo, length, stride):
        step = stride * 2
        if step < length:
            merge(lo, length, step)
            merge(lo + stride, length, step)
            pairs.extend((i, i + stride) for i in range(lo + stride, lo + length - stride, step))
        else:
            pairs.append((lo, lo + stride))

    def sort(lo, length):
        if length > 1:
            sort(lo, length // 2)
            sort(lo + length // 2, length // 2)
            merge(lo, length, 1)

    sort(0, n)
    return pairs


_SORT16 = _odd_even_merge_sort(PEER_NKEYS // V7X_SUBLANES)


def _peer_sel_kernel(st_ref, cnt_ref, e1_ref, rank2_ref, e2_ref):
    n_top = PEER_TOPK + 1
    not_ranked = float(PEER_NKEYS - 1)
    n_slabs = PEER_NKEYS // V7X_SUBLANES

    def top_values(x):
        cols = [x[v * V7X_SUBLANES:(v + 1) * V7X_SUBLANES, :] for v in range(n_slabs)]
        for i, j in _SORT16:
            cols[i], cols[j] = jnp.maximum(cols[i], cols[j]), jnp.minimum(cols[i], cols[j])
        cols.append(jnp.full(cols[0].shape, NEG_INF, F32))
        vals = []
        for r in range(n_top):
            m = jnp.max(cols[0], axis=0, keepdims=True)
            vals.append(m)
            hit = cols[0] == m
            for k in range(min(n_slabs, n_top - 1 - r)):
                cols[k] = jnp.where(hit, cols[k + 1], cols[k])
        return vals

    tops = [[top_values(st_ref[h, p]) for h in range(PEER_HEADS)] for p in range(2)]
    packed = [[jnp.concatenate([tops[p][h][r] for h in range(PEER_HEADS)], axis=0) for r in range(n_top)]
              for p in range(2)]
    pairs = [(i, j) for i in range(n_top) for j in range(n_top) if (i + 1) * (j + 1) <= n_top]
    sums = {ij: packed[0][ij[0]] + packed[1][ij[1]] for ij in pairs}
    cands = [sums[ij] for ij in pairs]
    best = []
    for _ in range(n_top):
        m = functools.reduce(jnp.maximum, cands)
        best.append(m)
        cands = [jnp.where(c == m, NEG_INF, c) for c in cands]
    thr = 0.5 * (best[PEER_TOPK - 1] + best[PEER_TOPK])
    z = functools.reduce(lambda a, b: a + b, [jnp.exp(b - best[0]) for b in best[:PEER_TOPK]])
    inv_z = 1.0 / z
    cnt_by_rank = [functools.reduce(lambda a, b: a + b,
                                    [jnp.where(sums[(i, j)] >= thr, 1.0, 0.0) for j in range(n_top) if (i, j) in sums])
                   for i in range(PEER_TOPK)]
    at_least = [functools.reduce(jnp.minimum, [jnp.where(cnt_by_rank[i] >= float(k), packed[0][i], -NEG_INF)
                                               for i in range(PEER_TOPK)])
                for k in range(1, PEER_TOPK + 1)]
    for h in range(PEER_HEADS):
        s1 = st_ref[h, 0]
        s2 = st_ref[h, 1]
        cnt = jnp.zeros(s1.shape, F32)
        for k in range(1, PEER_TOPK + 1):
            cnt = jnp.where(s1 >= at_least[k - 1][h:h + 1, :], float(k), cnt)
        cnt_ref[h] = cnt
        e1_ref[h] = jnp.exp(s1 - tops[0][h][0]) * inv_z[h:h + 1, :]
        rank2 = jnp.full(s2.shape, not_ranked, F32)
        for r in reversed(range(n_top)):
            rank2 = jnp.where(s2 >= tops[1][h][r], float(r), rank2)
        rank2_ref[h] = pltpu.bitcast(rank2.astype(BF16), jnp.uint32)
        e2_ref[h] = pltpu.bitcast(jnp.exp(s2 - tops[1][h][0]).astype(BF16), jnp.uint32)


def _peer_sel(st, tt):
    heads, _, n_keys, rows = st.shape
    tt = min(tt, rows)
    spec = pl.BlockSpec((heads, n_keys, tt), lambda i: (0, 0, i))
    word_spec = pl.BlockSpec((heads, n_keys // 2, tt), lambda i: (0, 0, i))
    f32_shape = jax.ShapeDtypeStruct((heads, n_keys, rows), F32)
    word_shape = jax.ShapeDtypeStruct((heads, n_keys // 2, rows), jnp.uint32)
    return pl.pallas_call(
        _peer_sel_kernel, grid=(rows // tt,),
        in_specs=[pl.BlockSpec((heads, 2, n_keys, tt), lambda i: (0, 0, 0, i))],
        out_specs=[spec, spec, word_spec, word_spec], out_shape=[f32_shape, f32_shape, word_shape, word_shape],
        compiler_params=_params("parallel"), name="peer_select",
    )(st)


def _gelu_tanh(x):
    k = 2.0 * 0.7978845608028654 * 1.4426950408889634
    return x / (1.0 + jnp.exp2(x * (-k - (k * 0.044715) * (x * x))))


def _peer_ffn_kernel(xn_ref, u_ref, vt_ref, cnt_ref, e1_ref, rank2_ref, e2_ref, h2_ref, fg_ref, o_ref,
                     acc_ref, act0_ref, act1_ref, gate_ref, *, n_groups, final_norm):
    s = pl.program_id(0)
    n_items = pl.num_programs(0) - 1
    ec, tt = act0_ref.shape
    nk = PEER_NKEYS
    pack = 2 * V7X_SUBLANES
    assert ec == V7X_SUBLANES * nk

    @pl.when(s == 0)
    def _():
        act0_ref[...] = jnp.zeros_like(act0_ref)
        act1_ref[...] = jnp.zeros_like(act1_ref)

    grp = jnp.clip(s - 1, 0, n_items - 1) % n_groups

    @pl.when(grp == 0)
    def _():
        acc_ref[...] = jnp.zeros_like(acc_ref)

    def gate_tile(act_old, gate_new, c, tls):
        lanes = [slice(tl * V7X_LANES, (tl + 1) * V7X_LANES) for tl in tls]
        w = [[None] * len(lanes) for _ in range(nk // pack)]
        for h in range(PEER_HEADS):
            cnt_b = [jnp.broadcast_to(cnt_ref[h, grp, c:c + 1, ts], (pack, V7X_LANES)).astype(BF16) for ts in lanes]
            e1_b = [jnp.broadcast_to(e1_ref[h, grp, c:c + 1, ts], (pack, V7X_LANES)).astype(BF16) for ts in lanes]
            for rg in range(nk // pack):
                ws = slice(rg * V7X_SUBLANES, (rg + 1) * V7X_SUBLANES)
                for k, ts in enumerate(lanes):
                    e2 = pltpu.bitcast(e2_ref[h, ws, ts], BF16)
                    rank2 = pltpu.bitcast(rank2_ref[h, ws, ts], BF16)
                    term = jnp.where(rank2 < cnt_b[k], e2, jnp.zeros_like(e2)) * e1_b[k]
                    w[rg][k] = term if w[rg][k] is None else w[rg][k] + term
        for rg in range(nk // pack):
            rows = slice(c * nk + rg * pack, c * nk + (rg + 1) * pack)
            words = slice((c * nk + rg * pack) // 2, (c * nk + (rg + 1) * pack) // 2)
            for k, ts in enumerate(lanes):
                gate_new[words, ts] = _pack_rows(_gelu_tanh(act_old[rows, ts]).astype(BF16) * w[rg][k])

    def stages(act_new, act_old):
        pair = 2 * V7X_LANES
        for p in range(tt // pair):
            cols = slice(p * pair, (p + 1) * pair)
            xn = _unpack_rows(xn_ref[p * pair // 2:(p + 1) * pair // 2, :])
            act_new[:, cols] = _dot_nt(_unpack_rows(u_ref[...]), xn)
            for c in range(V7X_SUBLANES):
                gate_tile(act_old, gate_ref, c, (2 * p, 2 * p + 1))
            acc_ref[:, cols] += _dot(_unpack_rows(vt_ref[...]), _unpack_rows(gate_ref[:, cols]))

    @pl.when(s % 2 == 0)
    def _():
        stages(act0_ref, act1_ref)

    @pl.when(s % 2 == 1)
    def _():
        stages(act1_ref, act0_ref)

    @pl.when((grp == n_groups - 1) & (s >= 1))
    def _():
        out = h2_ref[...] + acc_ref[...].T
        o_ref[...] = _rms(out, fg_ref[...]) if final_norm else out


def _peer_prep_kernel(u_ref, v_ref, uw_ref, vtw_ref):
    uw_ref[...] = _pack_rows(u_ref[...].astype(BF16))
    vtw_ref[...] = _pack_rows(v_ref[...].T.astype(BF16))


def _peer_prep(u, v, ec):
    n, d = u.shape
    return pl.pallas_call(
        _peer_prep_kernel, grid=(n // ec,),
        in_specs=[pl.BlockSpec((ec, d), lambda i: (i, 0)), pl.BlockSpec((ec, d), lambda i: (i, 0))],
        out_specs=[pl.BlockSpec((ec // 2, d), lambda i: (i, 0)), pl.BlockSpec((d // 2, ec), lambda i: (0, i))],
        out_shape=[jax.ShapeDtypeStruct((n // 2, d), jnp.uint32), jax.ShapeDtypeStruct((d // 2, n), jnp.uint32)],
        compiler_params=_params("parallel"), name="peer_prep",
    )(u, v)


def _peer_ffn(xn, u, vt, cnt, e1, rank2, e2, h2, f_g, final_norm, tt):
    rows, dm = h2.shape
    tt = min(tt, rows)
    heads, n_keys, _ = cnt.shape
    ec = V7X_SUBLANES * n_keys
    n_groups = 2 * u.shape[0] // ec
    n_items = (rows // tt) * n_groups
    item = lambda s, lag: jnp.clip(s - lag, 0, n_items - 1)
    word_spec = pl.BlockSpec((heads, n_keys // 2, tt), lambda s: (0, 0, item(s, 1) // n_groups))
    grp_spec = pl.BlockSpec((heads, n_keys // V7X_SUBLANES, V7X_SUBLANES, tt),
                            lambda s: (0, 0, 0, item(s, 1) // n_groups))
    cnt = cnt.reshape(heads, n_keys // V7X_SUBLANES, V7X_SUBLANES, rows)
    e1 = e1.reshape(heads, n_keys // V7X_SUBLANES, V7X_SUBLANES, rows)
    kern = functools.partial(_peer_ffn_kernel, n_groups=n_groups, final_norm=final_norm)
    return pl.pallas_call(
        kern, grid=(n_items + 1,),
        in_specs=[
            pl.BlockSpec((tt // 2, dm), lambda s: (item(s, 0) // n_groups, 0)),
            pl.BlockSpec((ec // 2, dm), lambda s: (item(s, 0) % n_groups, 0)),
            pl.BlockSpec((dm // 2, ec), lambda s: (0, item(s, 1) % n_groups)),
            grp_spec, grp_spec, word_spec, word_spec,
            pl.BlockSpec((tt, dm), lambda s: (item(s, 1) // n_groups, 0)),
            pl.BlockSpec((1, dm), lambda s: (0, 0)),
        ],
        out_specs=pl.BlockSpec((tt, dm), lambda s: (item(s, 1) // n_groups, 0)),
        out_shape=jax.ShapeDtypeStruct((rows, dm), F32),
        scratch_shapes=[pltpu.VMEM((dm, tt), F32), pltpu.VMEM((ec, tt), F32), pltpu.VMEM((ec, tt), F32),
                        pltpu.VMEM((ec // 2, tt), jnp.uint32)],
        compiler_params=_params("arbitrary"), name="peer_ffn",
    )(xn, u, vt, cnt, e1, rank2, e2, h2, f_g)


def _pad_head_cols(w, heads, width):
    lead = w.shape[:-1]
    w = w.reshape(lead + (heads, width))
    w = jnp.pad(w, [(0, 0)] * len(lead) + [(0, 0), (0, V7X_LANES - width)])
    return w.reshape(lead + (heads * V7X_LANES,))


def kernel(x, mem, mix_norm_g, w_in, gla_conv_w, gla_gate_w, gla_gate_b, gla_out_norm_g, sb_out_norm_g, w_out,
           xattn_norm_g, mem_norm_g, w_xq, w_xkv, w_xo, ffn_norm_g, peer_w_q, peer_sub_keys, peer_u, peer_v,
           final_norm_g):
    batch, seq, d = x.shape
    mem_len = mem.shape[1]
    rows = batch * seq
    depth = w_in.shape[0]
    qk = GLA_HEADS * GLA_DK
    vw = GLA_HEADS * GLA_DV
    sbw = SB_HEADS * SB_DH
    c1 = 2 * qk + vw
    c2 = c1 + vw
    c3 = c2 + GLA_LOWRANK
    c4 = c3 + sbw
    c5 = c4 + sbw

    h = x.reshape(rows, d)
    mem2 = mem.reshape(batch * mem_len, d)
    for l in range(depth):
        wl = w_in[l]
        w_conv = jnp.concatenate([_pad_head_cols(wl[:, :qk], GLA_HEADS, GLA_DK),
                                  _pad_head_cols(wl[:, qk:2 * qk], GLA_HEADS, GLA_DK),
                                  wl[:, 2 * qk:c1]], axis=1).astype(BF16)
        w_og = wl[:, c1:c2].astype(BF16)
        w_low = jnp.pad(wl[:, c2:c3], ((0, 0), (0, V7X_LANES - GLA_LOWRANK))).astype(BF16)
        w_sbq, w_sbk, w_sbv = (wl[:, a:b].astype(BF16) for a, b in ((c3, c4), (c4, c5), (c5, c5 + sbw)))
        cw = gla_conv_w[l]
        conv_w = jnp.concatenate([_pad_head_cols(cw[:, :qk], GLA_HEADS, GLA_DK),
                                  _pad_head_cols(cw[:, qk:2 * qk], GLA_HEADS, GLA_DK), cw[:, 2 * qk:]], axis=1)
        gate_w = jnp.pad(_pad_head_cols(gla_gate_w[l], GLA_HEADS, GLA_DK),
                         ((0, V7X_LANES - GLA_LOWRANK), (0, 0))).astype(BF16)
        gate_b = _pad_head_cols(gla_gate_b[l], GLA_HEADS, GLA_DK).reshape(1, -1)

        conv_in, og, low, sbq, sbk, sbv = _norm_matmul(
            h, mix_norm_g[l], [w_conv, w_og, w_low, w_sbq, w_sbk, w_sbv], tm=512)
        o_gla = _gla(conv_in, og, low, conv_w, gate_w, gate_b, gla_out_norm_g[l].reshape(1, -1), batch, seq)
        o_sb = _sb(sbq, sbk, sbv, batch, seq)

        wo = w_out[l].astype(BF16)
        h1, xq = _mix_out(h, o_gla, o_sb, sb_out_norm_g[l].reshape(1, -1), wo[:vw], wo[vw:],
                          xattn_norm_g[l].reshape(1, -1), w_xq[l].astype(BF16), tm=512)
        (kv,) = _norm_matmul(mem2, mem_norm_g[l], [w_xkv[l].astype(BF16)], tm=512)
        keys = peer_sub_keys[l].reshape(PEER_HEADS * 2, PEER_NKEYS, -1).astype(BF16)
        h2, xn, st = _xattn(xq, kv, h1, w_xo[l].astype(BF16), ffn_norm_g[l].reshape(1, -1),
                            peer_w_q[l].astype(BF16), keys, batch, seq, tm=512)
        st = st.reshape(PEER_HEADS, 2, PEER_NKEYS, rows)
        cnt, e1, rank2, e2 = _peer_sel(st, tt=256)
        u_words, vt_words = _peer_prep(peer_u[l], peer_v[l], ec=V7X_SUBLANES * PEER_NKEYS)
        h = _peer_ffn(xn, u_words, vt_words, cnt, e1, rank2, e2, h2,
                      final_norm_g.reshape(1, -1), final_norm=(l == depth - 1), tt=512)
    return h.reshape(batch, seq, d)
```

```python
import functools

import jax
import jax.numpy as jnp
from jax import lax
from jax.experimental import pallas as pl
from jax.experimental.pallas import tpu as pltpu

EPS = 1e-6
GLA_HEADS = 4
GLA_DK = 64
GLA_DV = 128
GLA_LOWRANK = 16
GLA_GATE_NORM = 16.0
GLA_CHUNK = 128
CONV_WIDTH = 4
SB_HEADS = 8
SB_DH = 64
SB_BLOCK = 256
SB_LOG_WEIGHT_FLOOR = -104.0
X_HEADS = 4
X_DH = 256
PEER_HEADS = 8
PEER_NKEYS = 128
PEER_TOPK = 16

V7X_LANES = 128
V7X_SUBLANES = 8
V7X_VMEM_LIMIT_BYTES = 52 * 1024 * 1024

F32 = jnp.float32
BF16 = jnp.bfloat16
NEG_INF = float("-inf")


def _params(*semantics, flags=None):
    return pltpu.CompilerParams(dimension_semantics=semantics, vmem_limit_bytes=V7X_VMEM_LIMIT_BYTES, flags=flags)


def _dot(a, b):
    return jnp.dot(a, b, preferred_element_type=F32)


def _dot_nt(a, b):
    return lax.dot_general(a, b, (((1,), (1,)), ((), ())), preferred_element_type=F32)


def _dot_tn(a, b):
    return lax.dot_general(a, b, (((0,), (0,)), ((), ())), preferred_element_type=F32)


def _rms(x, g):
    return x * lax.rsqrt(jnp.mean(x * x, axis=-1, keepdims=True) + EPS) * g


def _sigmoid(x):
    return 1.0 / (1.0 + jnp.exp(-x))


def _log_sigmoid(x):
    return jnp.minimum(x, 0.0) - jnp.log(1.0 + jnp.exp(-jnp.abs(x)))


def _split_bf16(x):
    hi = x.astype(BF16)
    lo = (x - hi.astype(F32)).astype(BF16)
    return hi, lo


def _pack_rows(x):
    return pltpu.bitcast(x, jnp.uint32)


def _unpack_rows(words):
    return pltpu.bitcast(words, BF16)


def _norm_matmul_kernel(x_ref, g_ref, *refs):
    n = len(refs) // 2
    xn = _rms(x_ref[...], g_ref[...]).astype(BF16)
    for w_ref, o_ref in zip(refs[:n], refs[n:]):
        o_ref[...] = _dot(xn, w_ref[...]).astype(o_ref.dtype)


def _norm_matmul(x, g, weights, tm):
    rows, d = x.shape
    tm = min(tm, rows)
    in_specs = [pl.BlockSpec((tm, d), lambda i: (i, 0)), pl.BlockSpec((1, d), lambda i: (0, 0))]
    in_specs += [pl.BlockSpec(w.shape, lambda i: (0, 0)) for w in weights]
    out_specs = [pl.BlockSpec((tm, w.shape[1]), lambda i: (i, 0)) for w in weights]
    out_shape = [jax.ShapeDtypeStruct((rows, w.shape[1]), BF16) for w in weights]
    return pl.pallas_call(
        _norm_matmul_kernel, grid=(rows // tm,), in_specs=in_specs, out_specs=out_specs,
        out_shape=out_shape, compiler_params=_params("parallel"), name="norm_matmul",
    )(x, g.reshape(1, d), *weights)


def _gla_kernel(x_ref, og_ref, low_ref, cw_ref, gw_ref, gb_ref, ng_ref, o_ref, halo_ref, st_ref, *, chunk):
    c = chunk
    seq = x_ref.shape[0]
    qk_w = GLA_HEADS * V7X_LANES
    halo_ref[...] = jnp.zeros_like(halo_ref)
    st_ref[...] = jnp.zeros_like(st_ref)
    ri = lax.broadcasted_iota(jnp.int32, (c, c), 0)
    ci = lax.broadcasted_iota(jnp.int32, (c, c), 1)
    causal = ri >= ci
    lmat = jnp.where(causal, 1.0, 0.0).astype(BF16)
    cw = cw_ref[...]
    ng = ng_ref[...]

    def body(n, carry):
        r0 = pl.multiple_of(n * c, c)
        cur = x_ref[pl.ds(r0, c), :].astype(F32)
        ext = jnp.concatenate([halo_ref[...], cur], axis=0)
        y = cur * cw[CONV_WIDTH - 1:CONV_WIDTH, :]
        for j in range(1, CONV_WIDTH):
            tap = cw[CONV_WIDTH - 1 - j:CONV_WIDTH - j, :]
            y = y + pltpu.roll(ext, j, 0)[V7X_SUBLANES:, :] * tap
        halo_ref[...] = cur[c - V7X_SUBLANES:, :]
        y = y * _sigmoid(y)

        gpre = _dot(low_ref[pl.ds(r0, c), :], gw_ref[...]) + gb_ref[...]
        g = _log_sigmoid(gpre) * (1.0 / GLA_GATE_NORM)
        g_hi, g_lo = _split_bf16(g)
        cum = _dot(lmat, g_hi) + _dot(lmat, g_lo)

        for h in range(GLA_HEADS):
            sl = slice(h * V7X_LANES, (h + 1) * V7X_LANES)
            q = y[:, sl] * (GLA_DK ** -0.5)
            k = y[:, qk_w + h * V7X_LANES:qk_w + (h + 1) * V7X_LANES]
            v = y[:, 2 * qk_w + h * GLA_DV:2 * qk_w + (h + 1) * GLA_DV].astype(BF16)
            cm = cum[:, sl]
            last = cm[c - 1:c, :]
            mid = cm[c // 2 - 1:c // 2, :]
            q_in = (q * jnp.exp(cm)).astype(BF16)
            q_mid = (q * jnp.exp(cm - mid)).astype(BF16)
            k_mid = (k * jnp.exp(mid - cm)).astype(BF16)
            k_end = (k * jnp.exp(last - cm)).astype(BF16)
            scores = jnp.where(causal, _dot_nt(q_mid, k_mid), 0.0)
            state = st_ref[h]
            o = _dot(scores.astype(BF16), v) + _dot_nt(q_in, state.astype(BF16))
            st_ref[h] = state * jnp.exp(last) + _dot_tn(v, k_end)
            og = og_ref[pl.ds(r0, c), h * GLA_DV:(h + 1) * GLA_DV].astype(F32)
            o_ref[pl.ds(r0, c), h * GLA_DV:(h + 1) * GLA_DV] = (_rms(o, ng) * (og * _sigmoid(og))).astype(o_ref.dtype)
        return carry

    lax.fori_loop(0, seq // c, body, 0)


def _gla(conv_in, og, low, conv_w, gate_w, gate_b, norm_g, batch, seq):
    w = conv_in.shape[1]
    vw = og.shape[1]
    qk_w = GLA_HEADS * V7X_LANES
    kern = functools.partial(_gla_kernel, chunk=GLA_CHUNK)
    return pl.pallas_call(
        kern, grid=(batch,),
        in_specs=[
            pl.BlockSpec((seq, w), lambda b: (b, 0)),
            pl.BlockSpec((seq, vw), lambda b: (b, 0)),
            pl.BlockSpec((seq, V7X_LANES), lambda b: (b, 0)),
            pl.BlockSpec((CONV_WIDTH, w), lambda b: (0, 0)),
            pl.BlockSpec((V7X_LANES, qk_w), lambda b: (0, 0)),
            pl.BlockSpec((1, qk_w), lambda b: (0, 0)),
            pl.BlockSpec((1, GLA_DV), lambda b: (0, 0)),
        ],
        out_specs=pl.BlockSpec((seq, vw), lambda b: (b, 0)),
        out_shape=jax.ShapeDtypeStruct((batch * seq, vw), BF16),
        scratch_shapes=[pltpu.VMEM((V7X_SUBLANES, w), F32), pltpu.VMEM((GLA_HEADS, GLA_DV, V7X_LANES), F32)],
        compiler_params=_params("parallel"), name="gla",
    )(conv_in, og, low, conv_w, gate_w, gate_b, norm_g)


def _sb_kernel(q_ref, k_ref, v_ref, o_ref, acc_ref, carry_ref, *, blk):
    qi = pl.program_id(2)
    scale = SB_DH ** -0.5
    lane = lax.broadcasted_iota(jnp.int32, (1, V7X_LANES), 1)
    q2 = q_ref[...]
    zero = jnp.zeros_like(q2)
    q_heads = (jnp.where(lane < SB_DH, q2, zero), jnp.where(lane >= SB_DH, q2, zero))
    ri = lax.broadcasted_iota(jnp.int32, (blk, blk), 0)
    ci = lax.broadcasted_iota(jnp.int32, (blk, blk), 1)
    strict = ci < ri
    rj = lax.broadcasted_iota(jnp.int32, (2 * blk, blk), 0) % blk
    cs = lax.broadcasted_iota(jnp.int32, (2 * blk, blk), 1)
    suffix = jnp.where(rj > cs, 1.0, 0.0).astype(BF16)
    def block(kb, carry, acc, diagonal=False, keep=None):
        k0 = pl.multiple_of(kb * blk, blk)
        kblk = k_ref[pl.ds(k0, blk), :]
        vblk = v_ref[pl.ds(k0, blk), :]
        new_carry, new_acc = [], []
        for hh in range(2):
            z = _dot_nt(q_heads[hh], kblk) * scale
            soft = jnp.log(1.0 + jnp.exp(-jnp.abs(z)))
            log_beta = jnp.minimum(z, 0.0) - soft
            log_rest = log_beta - z
            if diagonal:
                log_rest = jnp.where(strict, log_rest, 0.0)
            hi, lo = _split_bf16(log_rest)
            tail = _dot(jnp.concatenate([hi, lo], axis=1), suffix)
            log_w = log_beta + tail
            w = jnp.exp(log_w if carry[hh] is None else log_w + carry[hh])
            if diagonal:
                w = jnp.where(strict, w, 0.0)
            total = jnp.broadcast_to(log_rest[:, 0:1] + tail[:, 0:1], (blk, blk))
            if keep is not None:
                w = w * keep
                total = total * keep
            out = _dot(w.astype(BF16), vblk)
            new_acc.append(out if acc[hh] is None else acc[hh] + out)
            new_carry.append(total if carry[hh] is None else carry[hh] + total)
        return new_carry, new_acc

    carry, acc = block(qi, [None, None], [None, None], diagonal=True)
    carry, acc = block(jnp.maximum(qi - 1, 0), carry, acc, keep=(qi > 0).astype(F32))
    for hh in range(2):
        carry_ref[hh] = carry[hh]
        acc_ref[hh] = acc[hh]

    def live():
        return jnp.max(carry_ref[:, :, :V7X_LANES]) > SB_LOG_WEIGHT_FLOOR

    def body(state):
        step, _ = state
        carry, acc = block(qi - 1 - step, [carry_ref[0], carry_ref[1]], [acc_ref[0], acc_ref[1]])
        for hh in range(2):
            carry_ref[hh] = carry[hh]
            acc_ref[hh] = acc[hh]
        return step + 1, live()

    lax.while_loop(lambda state: (state[0] < qi) & state[1], body, (jnp.int32(1), live()))
    o_ref[...] = jnp.where(lane < SB_DH, acc_ref[0], acc_ref[1]).astype(o_ref.dtype)


def _sb(q, k, v, batch, seq):
    blk = min(SB_BLOCK, seq)
    nq = seq // blk
    pairs = SB_HEADS // 2
    kern = functools.partial(_sb_kernel, blk=blk)
    return pl.pallas_call(
        kern, grid=(batch, pairs, nq),
        in_specs=[
            pl.BlockSpec((blk, V7X_LANES), lambda b, g, i: (b * nq + i, g)),
            pl.BlockSpec((seq, V7X_LANES), lambda b, g, i: (b, g)),
            pl.BlockSpec((seq, V7X_LANES), lambda b, g, i: (b, g)),
        ],
        out_specs=pl.BlockSpec((blk, V7X_LANES), lambda b, g, i: (b * nq + i, g)),
        out_shape=jax.ShapeDtypeStruct(q.shape, BF16),
        scratch_shapes=[pltpu.VMEM((2, blk, V7X_LANES), F32), pltpu.VMEM((2, blk, blk), F32)],
        compiler_params=_params("parallel", "parallel", "parallel"), name="stick_breaking",
    )(q, k, v)


def _mix_out_kernel(x_ref, og_ref, osb_ref, sbg_ref, wo_g_ref, wo_s_ref, xg_ref, wq_ref, h1_ref, q_ref):
    osb = _rms(osb_ref[...].astype(F32), sbg_ref[...]).astype(BF16)
    h1 = x_ref[...] + _dot(og_ref[...], wo_g_ref[...]) + _dot(osb, wo_s_ref[...])
    h1_ref[...] = h1
    q_ref[...] = _dot(_rms(h1, xg_ref[...]).astype(BF16), wq_ref[...]).astype(q_ref.dtype)


def _mix_out(x, o_gla, o_sb, sb_g, wo_g, wo_s, x_g, w_xq, tm):
    rows, d = x.shape
    tm = min(tm, rows)
    row = lambda w: pl.BlockSpec((tm, w), lambda i: (i, 0))
    full = lambda a: pl.BlockSpec(a.shape, lambda i: (0, 0))
    return pl.pallas_call(
        _mix_out_kernel, grid=(rows // tm,),
        in_specs=[row(d), row(o_gla.shape[1]), row(o_sb.shape[1]), full(sb_g), full(wo_g), full(wo_s), full(x_g), full(w_xq)],
        out_specs=[row(d), row(w_xq.shape[1])],
        out_shape=[jax.ShapeDtypeStruct((rows, d), F32), jax.ShapeDtypeStruct((rows, w_xq.shape[1]), BF16)],
        compiler_params=_params("parallel"), name="mix_out",
    )(x, o_gla, o_sb, sb_g, wo_g, wo_s, x_g, w_xq)


def _xattn_kernel(q_ref, kv_ref, h1_ref, wo_ref, fg_ref, wpq_ref, keys_ref, h2_ref, xn_ref, st_ref):
    xw = X_HEADS * X_DH
    outs = []
    for h in range(X_HEADS):
        qh = q_ref[:, h * X_DH:(h + 1) * X_DH]
        kh = kv_ref[:, h * X_DH:(h + 1) * X_DH]
        vh = kv_ref[:, xw + h * X_DH:xw + (h + 1) * X_DH]
        s = _dot_nt(qh, kh) * (X_DH ** -0.5)
        p = jnp.exp(s - jnp.max(s, axis=-1, keepdims=True))
        o = _dot(p.astype(BF16), vh) / jnp.sum(p, axis=-1, keepdims=True)
        outs.append(o.astype(BF16))
    h2 = h1_ref[...] + _dot(jnp.concatenate(outs, axis=-1), wo_ref[...])
    h2_ref[...] = h2
    xn = _rms(h2, fg_ref[...]).astype(BF16)
    xn_ref[...] = _pack_rows(xn)
    pq = _dot(xn, wpq_ref[...])
    half = keys_ref.shape[2]
    for hp in range(keys_ref.shape[0]):
        st_ref[hp] = _dot_nt(keys_ref[hp], pq[:, hp * half:(hp + 1) * half].astype(BF16))


def _xattn(q, kv, h1, w_xo, f_g, w_pq, keys, batch, seq, tm):
    rows, d = h1.shape
    tm = min(tm, seq)
    nt = seq // tm
    mem_len = kv.shape[0] // batch
    row = lambda w: pl.BlockSpec((tm, w), lambda b, i: (b * nt + i, 0))
    full = lambda a: pl.BlockSpec(a.shape, lambda b, i: (0,) * a.ndim)
    n_hp, n_keys, _ = keys.shape
    return pl.pallas_call(
        _xattn_kernel, grid=(batch, nt),
        in_specs=[row(q.shape[1]), pl.BlockSpec((mem_len, kv.shape[1]), lambda b, i: (b, 0)), row(d),
                  full(w_xo), full(f_g), full(w_pq), full(keys)],
        out_specs=[row(d), pl.BlockSpec((tm // 2, d), lambda b, i: (b * nt + i, 0)),
                   pl.BlockSpec((n_hp, n_keys, tm), lambda b, i: (0, 0, b * nt + i))],
        out_shape=[jax.ShapeDtypeStruct((rows, d), F32), jax.ShapeDtypeStruct((rows // 2, d), jnp.uint32),
                   jax.ShapeDtypeStruct((n_hp, n_keys, rows), F32)],
        compiler_params=_params("parallel", "parallel"), name="xattn_peer_scores",
    )(q, kv, h1, w_xo, f_g, w_pq, keys)


def _odd_even_merge_sort(n):
    pairs = []

    def merge(lo, length, stride):
        step = stride * 2
        if step < length:
            merge(lo, length, step)
            merge(lo + stride, length, step)
            pairs.extend((i, i + stride) for i in range(lo + stride, lo + length - stride, step))
        else:
            pairs.append((lo, lo + stride))

    def sort(lo, length):
        if length > 1:
            sort(lo, length // 2)
            sort(lo + length // 2, length // 2)
            merge(lo, length, 1)

    sort(0, n)
    return pairs


_SORT16 = _odd_even_merge_sort(PEER_NKEYS // V7X_SUBLANES)


def _peer_sel_kernel(st_ref, cnt_ref, e1_ref, rank2_ref, e2_ref):
    n_top = PEER_TOPK + 1
    not_ranked = float(PEER_NKEYS - 1)
    n_slabs = PEER_NKEYS // V7X_SUBLANES

    def top_values(x):
        cols = [x[v * V7X_SUBLANES:(v + 1) * V7X_SUBLANES, :] for v in range(n_slabs)]
        for i, j in _SORT16:
            cols[i], cols[j] = jnp.maximum(cols[i], cols[j]), jnp.minimum(cols[i], cols[j])
        cols.append(jnp.full(cols[0].shape, NEG_INF, F32))
        vals = []
        for r in range(n_top):
            m = jnp.max(cols[0], axis=0, keepdims=True)
            vals.append(m)
            hit = cols[0] == m
            for k in range(min(n_slabs, n_top - 1 - r)):
                cols[k] = jnp.where(hit, cols[k + 1], cols[k])
        return vals

    tops = [[top_values(st_ref[h, p]) for h in range(PEER_HEADS)] for p in range(2)]
    packed = [[jnp.concatenate([tops[p][h][r] for h in range(PEER_HEADS)], axis=0) for r in range(n_top)]
              for p in range(2)]
    pairs = [(i, j) for i in range(n_top) for j in range(n_top) if (i + 1) * (j + 1) <= n_top]
    sums = {ij: packed[0][ij[0]] + packed[1][ij[1]] for ij in pairs}
    cands = [sums[ij] for ij in pairs]
    best = []
    for _ in range(n_top):
        m = functools.reduce(jnp.maximum, cands)
        best.append(m)
        cands = [jnp.where(c == m, NEG_INF, c) for c in cands]
    thr = 0.5 * (best[PEER_TOPK - 1] + best[PEER_TOPK])
    z = functools.reduce(lambda a, b: a + b, [jnp.exp(b - best[0]) for b in best[:PEER_TOPK]])
    inv_z = 1.0 / z
    cnt_by_rank = [functools.reduce(lambda a, b: a + b,
                                    [jnp.where(sums[(i, j)] >= thr, 1.0, 0.0) for j in range(n_top) if (i, j) in sums])
                   for i in range(PEER_TOPK)]
    at_least = [functools.reduce(jnp.minimum, [jnp.where(cnt_by_rank[i] >= float(k), packed[0][i], -NEG_INF)
                                               for i in range(PEER_TOPK)])
                for k in range(1, PEER_TOPK + 1)]
    for h in range(PEER_HEADS):
        s1 = st_ref[h, 0]
        s2 = st_ref[h, 1]
        cnt = jnp.zeros(s1.shape, F32)
        for k in range(1, PEER_TOPK + 1):
            cnt = jnp.where(s1 >= at_least[k - 1][h:h + 1, :], float(k), cnt)
        cnt_ref[h] = cnt
        e1_ref[h] = jnp.exp(s1 - tops[0][h][0]) * inv_z[h:h + 1, :]
        rank2 = jnp.full(s2.shape, not_ranked, F32)
        for r in reversed(range(n_top)):
            rank2 = jnp.where(s2 >= tops[1][h][r], float(r), rank2)
        rank2_ref[h] = pltpu.bitcast(rank2.astype(BF16), jnp.uint32)
        e2_ref[h] = pltpu.bitcast(jnp.exp(s2 - tops[1][h][0]).astype(BF16), jnp.uint32)


def _peer_sel(st, tt):
    heads, _, n_keys, rows = st.shape
    tt = min(tt, rows)
    spec = pl.BlockSpec((heads, n_keys, tt), lambda i: (0, 0, i))
    word_spec = pl.BlockSpec((heads, n_keys // 2, tt), lambda i: (0, 0, i))
    f32_shape = jax.ShapeDtypeStruct((heads, n_keys, rows), F32)
    word_shape = jax.ShapeDtypeStruct((heads, n_keys // 2, rows), jnp.uint32)
    return pl.pallas_call(
        _peer_sel_kernel, grid=(rows // tt,),
        in_specs=[pl.BlockSpec((heads, 2, n_keys, tt), lambda i: (0, 0, 0, i))],
        out_specs=[spec, spec, word_spec, word_spec], out_shape=[f32_shape, f32_shape, word_shape, word_shape],
        compiler_params=_params("parallel"), name="peer_select",
    )(st)


def _gelu_tanh(x):
    k = 2.0 * 0.7978845608028654 * 1.4426950408889634
    return x / (1.0 + jnp.exp2(x * (-k - (k * 0.044715) * (x * x))))


def _peer_ffn_kernel(xn_ref, u_ref, vt_ref, cnt_ref, e1_ref, rank2_ref, e2_ref, h2_ref, fg_ref, o_ref,
                     acc_ref, act0_ref, act1_ref, gate_ref, *, n_groups, final_norm):
    s = pl.program_id(0)
    n_items = pl.num_programs(0) - 1
    ec, tt = act0_ref.shape
    nk = PEER_NKEYS
    pack = 2 * V7X_SUBLANES
    assert ec == V7X_SUBLANES * nk

    @pl.when(s == 0)
    def _():
        act0_ref[...] = jnp.zeros_like(act0_ref)
        act1_ref[...] = jnp.zeros_like(act1_ref)

    grp = jnp.clip(s - 1, 0, n_items - 1) % n_groups

    @pl.when(grp == 0)
    def _():
        acc_ref[...] = jnp.zeros_like(acc_ref)

    def gate_tile(act_old, gate_new, c, tls):
        lanes = [slice(tl * V7X_LANES, (tl + 1) * V7X_LANES) for tl in tls]
        w = [[None] * len(lanes) for _ in range(nk // pack)]
        for h in range(PEER_HEADS):
            cnt_b = [jnp.broadcast_to(cnt_ref[h, grp, c:c + 1, ts], (pack, V7X_LANES)).astype(BF16) for ts in lanes]
            e1_b = [jnp.broadcast_to(e1_ref[h, grp, c:c + 1, ts], (pack, V7X_LANES)).astype(BF16) for ts in lanes]
            for rg in range(nk // pack):
                ws = slice(rg * V7X_SUBLANES, (rg + 1) * V7X_SUBLANES)
                for k, ts in enumerate(lanes):
                    e2 = pltpu.bitcast(e2_ref[h, ws, ts], BF16)
                    rank2 = pltpu.bitcast(rank2_ref[h, ws, ts], BF16)
                    term = jnp.where(rank2 < cnt_b[k], e2, jnp.zeros_like(e2)) * e1_b[k]
                    w[rg][k] = term if w[rg][k] is None else w[rg][k] + term
        for rg in range(nk // pack):
            rows = slice(c * nk + rg * pack, c * nk + (rg + 1) * pack)
            words = slice((c * nk + rg * pack) // 2, (c * nk + (rg + 1) * pack) // 2)
            for k, ts in enumerate(lanes):
                gate_new[words, ts] = _pack_rows(_gelu_tanh(act_old[rows, ts]).astype(BF16) * w[rg][k])

    def stages(act_new, act_old):
        pair = 2 * V7X_LANES
        for p in range(tt // pair):
            cols = slice(p * pair, (p + 1) * pair)
            xn = _unpack_rows(xn_ref[p * pair // 2:(p + 1) * pair // 2, :])
            act_new[:, cols] = _dot_nt(_unpack_rows(u_ref[...]), xn)
            for c in range(V7X_SUBLANES):
                gate_tile(act_old, gate_ref, c, (2 * p, 2 * p + 1))
            acc_ref[:, cols] += _dot(_unpack_rows(vt_ref[...]), _unpack_rows(gate_ref[:, cols]))

    @pl.when(s % 2 == 0)
    def _():
        stages(act0_ref, act1_ref)

    @pl.when(s % 2 == 1)
    def _():
        stages(act1_ref, act0_ref)

    @pl.when((grp == n_groups - 1) & (s >= 1))
    def _():
        out = h2_ref[...] + acc_ref[...].T
        o_ref[...] = _rms(out, fg_ref[...]) if final_norm else out


def _peer_prep_kernel(u_ref, v_ref, uw_ref, vtw_ref):
    uw_ref[...] = _pack_rows(u_ref[...].astype(BF16))
    vtw_ref[...] = _pack_rows(v_ref[...].T.astype(BF16))


def _peer_prep(u, v, ec):
    n, d = u.shape
    return pl.pallas_call(
        _peer_prep_kernel, grid=(n // ec,),
        in_specs=[pl.BlockSpec((ec, d), lambda i: (i, 0)), pl.BlockSpec((ec, d), lambda i: (i, 0))],
        out_specs=[pl.BlockSpec((ec // 2, d), lambda i: (i, 0)), pl.BlockSpec((None, d // 2, ec), lambda i: (i, 0, 0))],
        out_shape=[jax.ShapeDtypeStruct((n // 2, d), jnp.uint32), jax.ShapeDtypeStruct((n // ec, d // 2, ec), jnp.uint32)],
        compiler_params=_params("parallel"), name="peer_prep",
    )(u, v)


def _peer_ffn(xn, u, vt, cnt, e1, rank2, e2, h2, f_g, final_norm, tt):
    rows, dm = h2.shape
    tt = min(tt, rows)
    heads, n_keys, _ = cnt.shape
    ec = V7X_SUBLANES * n_keys
    n_groups = 2 * u.shape[0] // ec
    n_items = (rows // tt) * n_groups
    item = lambda s, lag: jnp.clip(s - lag, 0, n_items - 1)
    word_spec = pl.BlockSpec((heads, n_keys // 2, tt), lambda s: (0, 0, item(s, 1) // n_groups))
    grp_spec = pl.BlockSpec((heads, n_keys // V7X_SUBLANES, V7X_SUBLANES, tt),
                            lambda s: (0, 0, 0, item(s, 1) // n_groups))
    cnt = cnt.reshape(heads, n_keys // V7X_SUBLANES, V7X_SUBLANES, rows)
    e1 = e1.reshape(heads, n_keys // V7X_SUBLANES, V7X_SUBLANES, rows)
    kern = functools.partial(_peer_ffn_kernel, n_groups=n_groups, final_norm=final_norm)
    return pl.pallas_call(
        kern, grid=(n_items + 1,),
        in_specs=[
            pl.BlockSpec((tt // 2, dm), lambda s: (item(s, 0) // n_groups, 0)),
            pl.BlockSpec((ec // 2, dm), lambda s: (item(s, 0) % n_groups, 0)),
            pl.BlockSpec((None, dm // 2, ec), lambda s: (item(s, 1) % n_groups, 0, 0)),
            grp_spec, grp_spec, word_spec, word_spec,
            pl.BlockSpec((tt, dm), lambda s: (item(s, 1) // n_groups, 0)),
            pl.BlockSpec((1, dm), lambda s: (0, 0)),
        ],
        out_specs=pl.BlockSpec((tt, dm), lambda s: (item(s, 1) // n_groups, 0)),
        out_shape=jax.ShapeDtypeStruct((rows, dm), F32),
        scratch_shapes=[pltpu.VMEM((dm, tt), F32), pltpu.VMEM((ec, tt), F32), pltpu.VMEM((ec, tt), F32),
                        pltpu.VMEM((ec // 2, tt), jnp.uint32)],
        compiler_params=_params("arbitrary"), name="peer_ffn",
    )(xn, u, vt, cnt, e1, rank2, e2, h2, f_g)


def _pad_head_cols(w, heads, width):
    lead = w.shape[:-1]
    w = w.reshape(lead + (heads, width))
    w = jnp.pad(w, [(0, 0)] * len(lead) + [(0, 0), (0, V7X_LANES - width)])
    return w.reshape(lead + (heads * V7X_LANES,))


def kernel(x, mem, mix_norm_g, w_in, gla_conv_w, gla_gate_w, gla_gate_b, gla_out_norm_g, sb_out_norm_g, w_out,
           xattn_norm_g, mem_norm_g, w_xq, w_xkv, w_xo, ffn_norm_g, peer_w_q, peer_sub_keys, peer_u, peer_v,
           final_norm_g):
    batch, seq, d = x.shape
    mem_len = mem.shape[1]
    rows = batch * seq
    depth = w_in.shape[0]
    qk = GLA_HEADS * GLA_DK
    vw = GLA_HEADS * GLA_DV
    sbw = SB_HEADS * SB_DH
    c1 = 2 * qk + vw
    c2 = c1 + vw
    c3 = c2 + GLA_LOWRANK
    c4 = c3 + sbw
    c5 = c4 + sbw

    h = x.reshape(rows, d)
    mem2 = mem.reshape(batch * mem_len, d)
    for l in range(depth):
        wl = w_in[l]
        w_conv = jnp.concatenate([_pad_head_cols(wl[:, :qk], GLA_HEADS, GLA_DK),
                                  _pad_head_cols(wl[:, qk:2 * qk], GLA_HEADS, GLA_DK),
                                  wl[:, 2 * qk:c1]], axis=1).astype(BF16)
        w_og = wl[:, c1:c2].astype(BF16)
        w_low = jnp.pad(wl[:, c2:c3], ((0, 0), (0, V7X_LANES - GLA_LOWRANK))).astype(BF16)
        w_sbq, w_sbk, w_sbv = (wl[:, a:b].astype(BF16) for a, b in ((c3, c4), (c4, c5), (c5, c5 + sbw)))
        cw = gla_conv_w[l]
        conv_w = jnp.concatenate([_pad_head_cols(cw[:, :qk], GLA_HEADS, GLA_DK),
                                  _pad_head_cols(cw[:, qk:2 * qk], GLA_HEADS, GLA_DK), cw[:, 2 * qk:]], axis=1)
        gate_w = jnp.pad(_pad_head_cols(gla_gate_w[l], GLA_HEADS, GLA_DK),
                         ((0, V7X_LANES - GLA_LOWRANK), (0, 0))).astype(BF16)
        gate_b = _pad_head_cols(gla_gate_b[l], GLA_HEADS, GLA_DK).reshape(1, -1)

        conv_in, og, low, sbq, sbk, sbv = _norm_matmul(
            h, mix_norm_g[l], [w_conv, w_og, w_low, w_sbq, w_sbk, w_sbv], tm=512)
        o_gla = _gla(conv_in, og, low, conv_w, gate_w, gate_b, gla_out_norm_g[l].reshape(1, -1), batch, seq)
        o_sb = _sb(sbq, sbk, sbv, batch, seq)

        wo = w_out[l].astype(BF16)
        h1, xq = _mix_out(h, o_gla, o_sb, sb_out_norm_g[l].reshape(1, -1), wo[:vw], wo[vw:],
                          xattn_norm_g[l].reshape(1, -1), w_xq[l].astype(BF16), tm=512)
        (kv,) = _norm_matmul(mem2, mem_norm_g[l], [w_xkv[l].astype(BF16)], tm=512)
        keys = peer_sub_keys[l].reshape(PEER_HEADS * 2, PEER_NKEYS, -1).astype(BF16)
        h2, xn, st = _xattn(xq, kv, h1, w_xo[l].astype(BF16), ffn_norm_g[l].reshape(1, -1),
                            peer_w_q[l].astype(BF16), keys, batch, seq, tm=512)
        st = st.reshape(PEER_HEADS, 2, PEER_NKEYS, rows)
        cnt, e1, rank2, e2 = _peer_sel(st, tt=256)
        u_words, vt_words = _peer_prep(peer_u[l], peer_v[l], ec=V7X_SUBLANES * PEER_NKEYS)
        h = _peer_ffn(xn, u_words, vt_words, cnt, e1, rank2, e2, h2,
                      final_norm_g.reshape(1, -1), final_norm=(l == depth - 1), tt=512)
    return h.reshape(batch, seq, d)
```
